```python
import math
import jax
import jax.numpy as jnp
from jax import lax
import numpy as np

D_MODEL = 1024
BATCH = 8
SEQ = 2048
DEPTH = 2
DEC_BATCH = 128
DEC_SEQ = 8
PAST_LEN = 2048
PAGE_SIZE = 128

N_HEADS = 8
HEAD_DIM = 64
D_ATTN = N_HEADS * HEAD_DIM
MOBA_BLOCK = 256
MOBA_TOPK = 3
PROMPT_Q_BLOCK = 64
SAMPLE_Q_BLOCK = 1
D_CONV = D_MODEL // 2
CONV_WIDTH = 31
N_IN = 2 * D_CONV + 3 * D_ATTN + 2 * D_MODEL
N_EXPERTS = 32
TOP_K = 4
D_EXPERT = D_MODEL
SWIGLU_ALPHA = 1.702
SWIGLU_LIMIT = 7.0
D_PLE = 256
DN_ALPHA = (2 * DEPTH) ** 0.25
DN_BETA = (8 * DEPTH) ** -0.25
LN_EPS = 1e-5

kernel_name = "hybrid_conv_moba_moe_decoder_step"


def layer_norm(x, g, b):
    xf = x.astype(jnp.float32)
    mu = jnp.mean(xf, axis=-1, keepdims=True)
    xc = xf - mu
    var = jnp.mean(jnp.square(xc), axis=-1, keepdims=True)
    y = xc * lax.rsqrt(var + LN_EPS)
    return (y * g.astype(jnp.float32) + b.astype(jnp.float32)).astype(x.dtype)


def alibi_slopes():
    return jnp.asarray([2.0 ** (-8.0 * (h + 1) / N_HEADS) for h in range(N_HEADS)], dtype=jnp.float32)


def moba_attention(q, k, v, q_pos, q_block):
    B, Q, H, Dh = q.shape
    L = k.shape[1]
    nb = -(-L // MOBA_BLOCK)
    pad = nb * MOBA_BLOCK - L
    kp = jnp.pad(k, ((0, 0), (0, pad), (0, 0), (0, 0)))
    vp = jnp.pad(v, ((0, 0), (0, pad), (0, 0), (0, 0)))
    kb = kp.reshape(B, nb, MOBA_BLOCK, H, Dh).transpose(0, 3, 1, 2, 4)
    vb = vp.reshape(B, nb, MOBA_BLOCK, H, Dh).transpose(0, 3, 1, 2, 4)
    kmean = jnp.mean(kb.astype(jnp.float32), axis=3)
    n_sel = min(MOBA_TOPK, nb)
    slopes = alibi_slopes()[None, :, None, None, None]
    b_idx = jnp.arange(B)[:, None, None, None]
    h_idx = jnp.arange(H)[None, :, None, None]
    offs = jnp.arange(MOBA_BLOCK, dtype=jnp.int32)
    blk_ids = jnp.arange(nb, dtype=jnp.int32)
    scale = HEAD_DIM ** -0.5

    def one_block(args):
        qc, pc = args
        qn = qc.shape[1]
        own = pc // MOBA_BLOCK
        gate = jnp.einsum('bqhd,bhnd->bhqn', qc.astype(jnp.float32), kmean)
        is_past = blk_ids[None, :] < own[:, None]
        gate = jnp.where(is_past[None, None], gate, -jnp.inf)
        _, sel = lax.top_k(gate, n_sel)
        own_b = jnp.broadcast_to(own[None, None, :, None], (B, H, qn, 1))
        blocks = jnp.concatenate([sel, own_b], axis=-1)
        blk_ok = jnp.concatenate([sel < own_b, jnp.ones_like(own_b, dtype=bool)], axis=-1)
        kg = kb[b_idx, h_idx, blocks]
        vg = vb[b_idx, h_idx, blocks]
        kpos = blocks[..., None] * MOBA_BLOCK + offs
        dist = (pc[None, None, :, None, None] - kpos).astype(jnp.float32)
        s = jnp.einsum('bqhd,bhqnkd->bhqnk', qc, kg).astype(jnp.float32) * scale - slopes * dist
        ok = blk_ok[..., None] & (dist >= 0)
        s = jnp.where(ok, s, -jnp.inf)
        ns = blocks.shape[-1]
        pr = jax.nn.softmax(s.reshape(B, H, qn, ns * MOBA_BLOCK), axis=-1)
        pr = pr.reshape(s.shape).astype(vg.dtype)
        return jnp.einsum('bhqnk,bhqnkd->bqhd', pr, vg)

    nc = Q // q_block
    qs = q.reshape(B, nc, q_block, H, Dh).transpose(1, 0, 2, 3, 4)
    ps = q_pos.reshape(nc, q_block)
    out = lax.map(one_block, (qs, ps))
    return out.transpose(1, 0, 2, 3, 4).reshape(B, Q, H * Dh)


def conv_module(u, hist, conv_w, conv_b, ln_g, ln_b):
    full = jnp.concatenate([hist, u], axis=1)
    y = lax.conv_general_dilated(full, conv_w[:, None, :], window_strides=(1,), padding='VALID',
                                 dimension_numbers=('NWC', 'WIO', 'NWC'),
                                 feature_group_count=D_CONV) + conv_b
    y = jax.nn.silu(layer_norm(y, ln_g, ln_b))
    return y, full[:, -(CONV_WIDTH - 1):]


def token_mixers(x, lp, conv_hist, past_k, past_v, q_pos, q_block):
    B, S, _ = x.shape
    z = x @ lp['w_in'] + lp['b_in']
    o = 0
    glu_val = z[..., o:o + D_CONV]; o += D_CONV
    glu_gate = z[..., o:o + D_CONV]; o += D_CONV
    q = z[..., o:o + D_ATTN].reshape(B, S, N_HEADS, HEAD_DIM); o += D_ATTN
    k = z[..., o:o + D_ATTN].reshape(B, S, N_HEADS, HEAD_DIM); o += D_ATTN
    v = z[..., o:o + D_ATTN].reshape(B, S, N_HEADS, HEAD_DIM); o += D_ATTN
    g_conv = z[..., o:o + D_MODEL]; o += D_MODEL
    g_attn = z[..., o:o + D_MODEL]
    u = glu_val * jax.nn.sigmoid(glu_gate)
    c, new_hist = conv_module(u, conv_hist, lp['conv_w'], lp['conv_b'], lp['conv_ln_g'], lp['conv_ln_b'])
    c = c @ lp['w_conv_out']
    k_all = jnp.concatenate([past_k, k], axis=1)
    v_all = jnp.concatenate([past_v, v], axis=1)
    a = moba_attention(q, k_all, v_all, q_pos, q_block) @ lp['w_attn_out']
    merged = jax.nn.sigmoid(g_conv) * c + jax.nn.sigmoid(g_attn) * a
    return merged @ lp['w_out'], k, v, new_hist


def routed_experts(x, router_w, router_b, w_gu, b_gu, w_down, b_down):
    logits = (x @ router_w + router_b).astype(jnp.float32)
    top_val, top_idx = lax.top_k(logits, TOP_K)
    gates = jax.nn.softmax(top_val, axis=-1)
    comb = jnp.sum(jax.nn.one_hot(top_idx, N_EXPERTS, dtype=jnp.float32) * gates[..., None], axis=1).astype(x.dtype)

    def body(e, acc):
        h = x @ w_gu[e] + b_gu[e]
        gate = jnp.minimum(h[:, :D_EXPERT], SWIGLU_LIMIT)
        up = jnp.clip(h[:, D_EXPERT:], -SWIGLU_LIMIT, SWIGLU_LIMIT)
        act = (up + 1.0) * (gate * jax.nn.sigmoid(SWIGLU_ALPHA * gate))
        y = act @ w_down[e] + b_down[e]
        return acc + lax.dynamic_index_in_dim(comb, e, axis=1, keepdims=True) * y

    return lax.fori_loop(0, N_EXPERTS, body, jnp.zeros_like(x))


def decoder_layer(x, p_i, lp, conv_hist, past_k, past_v, q_pos, q_block):
    B, S, D = x.shape
    m, k, v, new_hist = token_mixers(x, lp, conv_hist, past_k, past_v, q_pos, q_block)
    x = layer_norm(DN_ALPHA * x + m, lp['ln1_g'], lp['ln1_b'])
    f = routed_experts(x.reshape(B * S, D), lp['router_w'], lp['router_b'], lp['exp_w_gu'],
                       lp['exp_b_gu'], lp['exp_w_down'], lp['exp_b_down']).reshape(B, S, D)
    x = layer_norm(DN_ALPHA * x + f, lp['ln2_g'], lp['ln2_b'])
    e = jax.nn.sigmoid(x @ lp['ple_gate_w'] + lp['ple_gate_b']) * (p_i @ lp['ple_w'])
    x = layer_norm(DN_ALPHA * x + e, lp['ln3_g'], lp['ln3_b'])
    return x, k, v, new_hist


def setup_inputs(seed: int = 0) -> dict:
    key = jax.random.key(seed)
    ks = jax.random.split(key, 40)
    f32 = jnp.float32
    nrm = lambda i, shape: jax.random.normal(ks[i], shape, dtype=f32)
    n_pages = PAST_LEN // PAGE_SIZE
    n_used = DEC_BATCH * n_pages
    n_phys = n_used + max(1, n_used // 4)
    page_table = jax.random.permutation(ks[0], n_phys)[:n_used].reshape(DEC_BATCH, n_pages).astype(jnp.int32)

    v_lo = 2 * D_CONV + 2 * D_ATTN
    w_in = nrm(9, (DEPTH, D_MODEL, N_IN)) * D_MODEL ** -0.5
    w_in = w_in.at[:, :, v_lo:v_lo + D_ATTN].multiply(DN_BETA)
    return {
        "x_prompt": nrm(1, (BATCH, SEQ, D_MODEL)),
        "x_sample": nrm(2, (DEC_BATCH, DEC_SEQ, D_MODEL)),
        "p_prompt": nrm(3, (DEPTH, BATCH, SEQ, D_PLE)),
        "p_sample": nrm(4, (DEPTH, DEC_BATCH, DEC_SEQ, D_PLE)),
        "cache_k": nrm(5, (DEPTH, n_phys, PAGE_SIZE, N_HEADS, HEAD_DIM)),
        "cache_v": 0.5 * nrm(6, (DEPTH, n_phys, PAGE_SIZE, N_HEADS, HEAD_DIM)),
        "state_conv": 0.5 * nrm(7, (DEPTH, DEC_BATCH, CONV_WIDTH - 1, D_CONV)),
        "page_table": page_table,
        "w_in": w_in,
        "b_in": 0.01 * nrm(10, (DEPTH, N_IN)),
        "conv_w": nrm(11, (DEPTH, CONV_WIDTH, D_CONV)) * CONV_WIDTH ** -0.5,
        "conv_b": 0.01 * nrm(12, (DEPTH, D_CONV)),
        "conv_ln_g": 1.0 + 0.01 * nrm(13, (DEPTH, D_CONV)),
        "conv_ln_b": 0.01 * nrm(14, (DEPTH, D_CONV)),
        "w_conv_out": nrm(15, (DEPTH, D_CONV, D_MODEL)) * D_CONV ** -0.5 * DN_BETA,
        "w_attn_out": nrm(16, (DEPTH, D_ATTN, D_MODEL)) * D_ATTN ** -0.5 * DN_BETA,
        "w_out": nrm(17, (DEPTH, D_MODEL, D_MODEL)) * D_MODEL ** -0.5 * DN_BETA,
        "ln1_g": 1.0 + 0.01 * nrm(18, (DEPTH, D_MODEL)),
        "ln1_b": 0.01 * nrm(19, (DEPTH, D_MODEL)),
        "router_w": nrm(20, (DEPTH, D_MODEL, N_EXPERTS)) * D_MODEL ** -0.5,
        "router_b": 0.01 * nrm(21, (DEPTH, N_EXPERTS)),
        "exp_w_gu": nrm(22, (DEPTH, N_EXPERTS, D_MODEL, 2 * D_EXPERT)) * D_MODEL ** -0.5,
        "exp_b_gu": 0.01 * nrm(23, (DEPTH, N_EXPERTS, 2 * D_EXPERT)),
        "exp_w_down": nrm(24, (DEPTH, N_EXPERTS, D_EXPERT, D_MODEL)) * D_EXPERT ** -0.5 * DN_BETA,
        "exp_b_down": 0.01 * nrm(25, (DEPTH, N_EXPERTS, D_MODEL)),
        "ln2_g": 1.0 + 0.01 * nrm(26, (DEPTH, D_MODEL)),
        "ln2_b": 0.01 * nrm(27, (DEPTH, D_MODEL)),
        "ple_w": nrm(28, (DEPTH, D_PLE, D_MODEL)) * D_PLE ** -0.5 * DN_BETA,
        "ple_gate_w": nrm(29, (DEPTH, D_MODEL, D_MODEL)) * D_MODEL ** -0.5,
        "ple_gate_b": 0.01 * nrm(30, (DEPTH, D_MODEL)),
        "ln3_g": 1.0 + 0.01 * nrm(31, (DEPTH, D_MODEL)),
        "ln3_b": 0.01 * nrm(32, (DEPTH, D_MODEL)),
    }


def reference(x_prompt, x_sample, p_prompt, p_sample, cache_k, cache_v, state_conv, page_table,
              w_in, b_in, conv_w, conv_b, conv_ln_g, conv_ln_b, w_conv_out, w_attn_out, w_out,
              ln1_g, ln1_b, router_w, router_b, exp_w_gu, exp_b_gu, exp_w_down, exp_b_down,
              ln2_g, ln2_b, ple_w, ple_gate_w, ple_gate_b, ln3_g, ln3_b):
    bp, sp = x_prompt.shape[0], x_prompt.shape[1]
    bs, ss = x_sample.shape[0], x_sample.shape[1]
    n_pages = page_table.shape[1]
    past_len = n_pages * PAGE_SIZE
    pos_p = jnp.arange(sp, dtype=jnp.int32)
    pos_s = past_len + jnp.arange(ss, dtype=jnp.int32)
    hist0 = jnp.zeros((bp, CONV_WIDTH - 1, D_CONV), x_prompt.dtype)
    empty_kv = jnp.zeros((bp, 0, N_HEADS, HEAD_DIM), x_prompt.dtype)

    yp, ys = x_prompt, x_sample
    kp_l, vp_l, cp_l, ks_l, vs_l, cs_l = [], [], [], [], [], []
    for i in range(DEPTH):
        lp = {
            'w_in': w_in[i], 'b_in': b_in[i], 'conv_w': conv_w[i], 'conv_b': conv_b[i],
            'conv_ln_g': conv_ln_g[i], 'conv_ln_b': conv_ln_b[i], 'w_conv_out': w_conv_out[i],
            'w_attn_out': w_attn_out[i], 'w_out': w_out[i], 'ln1_g': ln1_g[i], 'ln1_b': ln1_b[i],
            'router_w': router_w[i], 'router_b': router_b[i], 'exp_w_gu': exp_w_gu[i],
            'exp_b_gu': exp_b_gu[i], 'exp_w_down': exp_w_down[i], 'exp_b_down': exp_b_down[i],
            'ln2_g': ln2_g[i], 'ln2_b': ln2_b[i], 'ple_w': ple_w[i], 'ple_gate_w': ple_gate_w[i],
            'ple_gate_b': ple_gate_b[i], 'ln3_g': ln3_g[i], 'ln3_b': ln3_b[i],
        }
        yp, kp, vp, cp = decoder_layer(yp, p_prompt[i], lp, hist0, empty_kv, empty_kv, pos_p, PROMPT_Q_BLOCK)
        past_k = cache_k[i][page_table].reshape(bs, past_len, N_HEADS, HEAD_DIM)
        past_v = cache_v[i][page_table].reshape(bs, past_len, N_HEADS, HEAD_DIM)
        ys, ks_, vs_, cs_ = decoder_layer(ys, p_sample[i], lp, state_conv[i], past_k, past_v, pos_s, SAMPLE_Q_BLOCK)
        kp_l.append(kp); vp_l.append(vp); cp_l.append(cp)
        ks_l.append(ks_); vs_l.append(vs_); cs_l.append(cs_)

    k_new_prompt = jnp.stack(kp_l)
    v_new_prompt = jnp.stack(vp_l)
    conv_new_prompt = jnp.stack(cp_l)
    k_new_sample = jnp.stack(ks_l)
    v_new_sample = jnp.stack(vs_l)
    conv_new_sample = jnp.stack(cs_l)
    return (yp, ys, k_new_prompt, v_new_prompt, conv_new_prompt, k_new_sample, v_new_sample, conv_new_sample)
```

```python
import functools

import jax
import jax.numpy as jnp
from jax import lax
from jax.experimental import pallas as pl
from jax.experimental.pallas import tpu as pltpu

F32 = jnp.float32
BF16 = jnp.bfloat16

HEAD_DIM = 64
MOBA_BLOCK = 256
MOBA_TOPK = 3
PAGE_SIZE = 128
TOP_K = 4
SWIGLU_ALPHA = 1.702
SWIGLU_LIMIT = 7.0
LN_EPS = 1e-5

LANES = 128
SUBLANES = 8
VMEM_LIMIT_BYTES = 56 * 1024 * 1024

NT_DIMS = (((1,), (1,)), ((), ()))


def _params(n_axes, vmem=None):
    return pltpu.CompilerParams(
        dimension_semantics=("arbitrary",) * n_axes,
        vmem_limit_bytes=VMEM_LIMIT_BYTES if vmem is None else vmem,
    )


def _pick_tile(cap, *sizes):
    t = cap
    while any(s % t for s in sizes):
        t //= 2
    return t


def _layer_norm(x, g, b):
    mu = jnp.mean(x, axis=-1, keepdims=True)
    xc = x - mu
    var = jnp.mean(xc * xc, axis=-1, keepdims=True)
    return xc * lax.rsqrt(var + LN_EPS) * g + b


def _split_bf16(a):
    hi = a.astype(BF16)
    lo = (a - hi.astype(F32)).astype(BF16)
    return hi, lo


def _dot3_nt(a, b):
    ah, al = _split_bf16(a)
    bh, bl = _split_bf16(b)
    f = lambda x, y: lax.dot_general(x, y, NT_DIMS, preferred_element_type=F32)
    return f(ah, bh) + (f(ah, bl) + f(al, bh))


def _bdot(a, b):
    return jnp.dot(a, b, preferred_element_type=F32)


def _inproj_body(x_ref, w_ref, b_ref, u_ref, q_ref, k_ref, v_ref, *, d_conv, d_attn):
    x = x_ref[...].astype(BF16)

    def proj(lo, n):
        return _bdot(x, w_ref[:, lo:lo + n]) + b_ref[:, lo:lo + n]

    u_ref[...] = proj(0, d_conv) * jax.nn.sigmoid(proj(d_conv, d_conv))
    o = 2 * d_conv
    q_ref[...] = proj(o, d_attn)
    k_ref[...] = proj(o + d_attn, d_attn)
    v_ref[...] = proj(o + 2 * d_attn, d_attn)


def _inproj(x, w_bf, b, ts, d_conv, d_attn):
    t, d = x.shape
    n = w_bf.shape[1]
    row = lambda i: (i, 0)
    fixed = lambda i: (0, 0)
    return pl.pallas_call(
        functools.partial(_inproj_body, d_conv=d_conv, d_attn=d_attn),
        grid=(t // ts,),
        in_specs=[pl.BlockSpec((ts, d), row), pl.BlockSpec((d, n), fixed), pl.BlockSpec((1, n), fixed)],
        out_specs=[pl.BlockSpec((ts, d_conv), row)] + [pl.BlockSpec((ts, d_attn), row)] * 3,
        out_shape=[jax.ShapeDtypeStruct((t, d_conv), F32)] + [jax.ShapeDtypeStruct((t, d_attn), F32)] * 3,
        compiler_params=_params(1),
        name="inproj",
    )(x, w_bf, b)


def _conv_prompt_body(u_ref, w_ref, cb_ref, g_ref, b_ref, o_ref, win_ref, *, ts, width, halo, rc):
    s = pl.program_id(1)
    c = u_ref.shape[1]

    @pl.when(s == 0)
    def _():
        win_ref[0:halo, :] = jnp.zeros((halo, c), F32)

    @pl.when(s > 0)
    def _():
        win_ref[0:halo, :] = win_ref[ts:ts + halo, :]

    win_ref[halo:halo + ts, :] = u_ref[...]
    off = halo - (width - 1)
    for r0 in range(0, ts, rc):
        acc = jnp.broadcast_to(cb_ref[...], (rc, c))
        for j in range(width):
            acc = acc + win_ref[pl.ds(r0 + off + j, rc), :] * w_ref[j:j + 1, :]
        y = _layer_norm(acc, g_ref[...], b_ref[...])
        o_ref[r0:r0 + rc, :] = y * jax.nn.sigmoid(y)


def _conv_prompt(u, conv_w, conv_b, ln_g, ln_b, bp, sp):
    t, c = u.shape
    width = conv_w.shape[0]
    halo = -(-(width - 1) // SUBLANES) * SUBLANES
    ts = _pick_tile(256, sp)
    assert ts >= halo
    ns = sp // ts
    tile = lambda b, s: (b * ns + s, 0)
    fixed = lambda b, s: (0, 0)
    return pl.pallas_call(
        functools.partial(_conv_prompt_body, ts=ts, width=width, halo=halo, rc=min(64, ts)),
        grid=(bp, ns),
        in_specs=[pl.BlockSpec((ts, c), tile), pl.BlockSpec((width, c), fixed)] + [pl.BlockSpec((1, c), fixed)] * 3,
        out_specs=pl.BlockSpec((ts, c), tile),
        out_shape=jax.ShapeDtypeStruct((bp * sp, c), F32),
        scratch_shapes=[pltpu.VMEM((halo + ts, c), F32)],
        compiler_params=_params(2),
        name="conv_prompt",
    )(u, conv_w, conv_b, ln_g, ln_b)


def _conv_sample_body(u_ref, h_ref, w_ref, cb_ref, g_ref, b_ref, o_ref, *, nb, dec, width):
    c = u_ref.shape[2]
    hist = width - 1

    def full_row(r):
        return h_ref[r] if r < hist else u_ref[r - hist]

    for t in range(dec):
        acc = jnp.broadcast_to(cb_ref[...], (nb, c))
        for j in range(width):
            acc = acc + full_row(t + j) * w_ref[j:j + 1, :]
        y = _layer_norm(acc, g_ref[...], b_ref[...])
        o_ref[t] = y * jax.nn.sigmoid(y)


def _conv_sample(u_t, hist_t, conv_w, conv_b, ln_g, ln_b):
    dec, bs, c = u_t.shape
    width = conv_w.shape[0]
    nb = min(32, bs)
    assert bs % nb == 0
    blockg = lambda g: (0, g, 0)
    fixed = lambda g: (0, 0)
    return pl.pallas_call(
        functools.partial(_conv_sample_body, nb=nb, dec=dec, width=width),
        grid=(bs // nb,),
        in_specs=[pl.BlockSpec((dec, nb, c), blockg), pl.BlockSpec((width - 1, nb, c), blockg),
                  pl.BlockSpec((width, c), fixed)] + [pl.BlockSpec((1, c), fixed)] * 3,
        out_specs=pl.BlockSpec((dec, nb, c), blockg),
        out_shape=jax.ShapeDtypeStruct((dec, bs, c), F32),
        compiler_params=_params(1),
        name="conv_sample",
    )(u_t, hist_t, conv_w, conv_b, ln_g, ln_b)


def _attn_prompt_body(slopes_ref, q_ref, k_ref, v_ref, o_ref, kmean_ref, *, blk, nb, n_sel, hd):
    hp = pl.program_id(1)
    qi = pl.program_id(2)

    @pl.when(qi == 0)
    def _():
        kmean_ref[...] = jnp.zeros(kmean_ref.shape, F32)
        for j in range(nb):
            kmean_ref[j:j + 1, :] = jnp.mean(k_ref[j * blk:(j + 1) * blk, :], axis=0, keepdims=True)

    lane = lax.broadcasted_iota(jnp.int32, (1, LANES), 1)
    row = lax.broadcasted_iota(jnp.int32, (blk, 1), 0)
    col = lax.broadcasted_iota(jnp.int32, (1, blk), 1)
    scale = HEAD_DIM ** -0.5
    neg_inf = -jnp.inf
    q = q_ref[...]
    outs = []
    for hh in range(LANES // hd):
        slope = slopes_ref[hp * (LANES // hd) + hh]
        qh = jnp.where(lane // hd == hh, q, 0.0)
        gate = _dot3_nt(qh, kmean_ref[...])
        qhb = qh.astype(BF16)

        def scores(j, qhb=qhb, slope=slope):
            kb = k_ref[pl.ds(pl.multiple_of(j * blk, blk), blk), :].astype(BF16)
            s = lax.dot_general(qhb, kb, NT_DIMS, preferred_element_type=F32) * scale
            dist = ((qi - j) * blk + row - col).astype(F32)
            return s - slope * dist, dist

        def pv(p, j):
            vb = v_ref[pl.ds(pl.multiple_of(j * blk, blk), blk), :].astype(BF16)
            return _bdot(p.astype(BF16), vb)

        s, dist = scores(qi)
        s = jnp.where(dist >= 0, s, neg_inf)
        m = jnp.max(s, axis=1, keepdims=True)
        p = jnp.exp(s - m)
        l = jnp.sum(p, axis=1, keepdims=True)
        acc = pv(p, qi)

        def body(j, carry, gate=gate, scores=scores, pv=pv):
            m, l, acc = carry
            gj = jnp.sum(jnp.where(lane == j, gate, 0.0), axis=1, keepdims=True)
            beats = (lane < qi) & ((gate > gj) | ((gate == gj) & (lane < j)))
            rank = jnp.sum(beats.astype(F32), axis=1, keepdims=True)
            s, _ = scores(j)
            s = jnp.where(rank < n_sel, s, neg_inf)
            m_new = jnp.maximum(m, jnp.max(s, axis=1, keepdims=True))
            a = jnp.exp(m - m_new)
            p = jnp.exp(s - m_new)
            return m_new, a * l + jnp.sum(p, axis=1, keepdims=True), a * acc + pv(p, j)

        m, l, acc = lax.fori_loop(0, qi, body, (m, l, acc))
        outs.append(acc / l)
    o = outs[0]
    for hh in range(1, len(outs)):
        o = jnp.where(lane // hd == hh, outs[hh], o)
    o_ref[...] = o


def _attn_prompt(slopes, q, k, v, bp, sp):
    t, da = q.shape
    blk = MOBA_BLOCK
    nb = sp // blk
    assert sp % blk == 0 and nb <= LANES and da % LANES == 0
    qtile = lambda b, hp, qi: (b * nb + qi, hp)
    seq = lambda b, hp, qi: (b, hp)
    return pl.pallas_call(
        functools.partial(_attn_prompt_body, blk=blk, nb=nb, n_sel=min(MOBA_TOPK, nb), hd=HEAD_DIM),
        grid=(bp, da // LANES, nb),
        in_specs=[pl.BlockSpec(memory_space=pltpu.SMEM), pl.BlockSpec((blk, LANES), qtile),
                  pl.BlockSpec((sp, LANES), seq), pl.BlockSpec((sp, LANES), seq)],
        out_specs=pl.BlockSpec((blk, LANES), qtile),
        out_shape=jax.ShapeDtypeStruct((bp * sp, da), F32),
        scratch_shapes=[pltpu.VMEM((LANES, LANES), F32)],
        compiler_params=_params(3),
        name="attn_prompt",
    )(slopes, q, k, v)


def _attn_sample_body(pt_ref, slopes_ref, q_ref, kn_ref, vn_ref, k0_ref, k1_ref, v0_ref, v1_ref, o_ref,
                      q2_ref, mo_ref, lo_ref, ao_ref, g_ref, ms_ref, ls_ref, acc_ref,
                      *, nh, hd, dec, nbk, blk, n_sel, past_len):
    del pt_ref
    j = pl.program_id(1)
    da = nh * hd
    nr = nh * dec
    scale = HEAD_DIM ** -0.5
    lane = lax.broadcasted_iota(jnp.int32, (1, LANES), 1)
    rowi = lax.broadcasted_iota(jnp.int32, (nr, 1), 0)
    slope = jnp.zeros((nr, 1), F32)
    for h in range(nh):
        slope = jnp.where(rowi // dec == h, slopes_ref[h], slope)
    tq = rowi % dec

    @pl.when(j == 0)
    def _():
        q = q_ref[...]
        hl = lax.broadcasted_iota(jnp.int32, (1, da), 1) // hd
        qbd = jnp.where(rowi // dec == hl, jnp.concatenate([q] * nh, axis=0), 0.0)
        qh, ql = _split_bf16(qbd)
        q2_ref[0:nr, :] = qh
        q2_ref[nr:2 * nr, :] = ql
        cols = []
        for t in range(dec):
            cols.append(jnp.sum(qbd * kn_ref[t:t + 1, :], axis=1, keepdims=True))
        s = jnp.concatenate(cols, axis=1) * scale
        tk = lax.broadcasted_iota(jnp.int32, (1, dec), 1)
        dist = (tq - tk).astype(F32)
        s = jnp.where(dist >= 0, s - slope * dist, -jnp.inf)
        m = jnp.max(s, axis=1, keepdims=True)
        p = jnp.exp(s - m)
        mo_ref[...] = m
        lo_ref[...] = jnp.sum(p, axis=1, keepdims=True)
        acc = jnp.zeros((nr, da), F32)
        for t in range(dec):
            acc = acc + p[:, t:t + 1] * vn_ref[t:t + 1, :]
        ao_ref[...] = acc
        g_ref[...] = jnp.zeros(g_ref.shape, F32)
        ms_ref[...] = jnp.zeros(ms_ref.shape, F32)
        ls_ref[...] = jnp.zeros(ls_ref.shape, F32)

    half = blk // 2
    kt = jnp.concatenate([k0_ref[...].reshape(da, half), k1_ref[...].reshape(da, half)], axis=1).astype(BF16)
    vt = jnp.concatenate([v0_ref[...].reshape(da, half), v1_ref[...].reshape(da, half)], axis=1).astype(BF16)
    s2 = _bdot(q2_ref[...], kt)
    raw = s2[0:nr, :] + s2[nr:2 * nr, :]
    gate = jnp.mean(raw, axis=1, keepdims=True)
    kpos = j * blk + lax.broadcasted_iota(jnp.int32, (1, blk), 1)
    dist = (past_len + tq - kpos).astype(F32)
    s = raw * scale - slope * dist
    m = jnp.max(s, axis=1, keepdims=True)
    p = jnp.exp(s - m)
    l = jnp.sum(p, axis=1, keepdims=True)
    acc_ref[j] = lax.dot_general(p.astype(BF16), vt, NT_DIMS, preferred_element_type=F32)
    g_ref[...] = jnp.where(lane == j, gate, g_ref[...])
    ms_ref[...] = jnp.where(lane == j, m, ms_ref[...])
    ls_ref[...] = jnp.where(lane == j, l, ls_ref[...])

    @pl.when(j == nbk - 1)
    def _():
        g = g_ref[...]
        rank = jnp.zeros(g.shape, F32)
        for jj in range(nbk):
            gj = g[:, jj:jj + 1]
            rank = rank + ((gj > g) | ((gj == g) & (jj < lane))).astype(F32)
        sel = (rank < n_sel) & (lane < nbk)
        ms = ms_ref[...]
        mo = mo_ref[...]
        mx = jnp.maximum(mo, jnp.max(jnp.where(sel, ms, -jnp.inf), axis=1, keepdims=True))
        w = jnp.where(sel, jnp.exp(ms - mx), 0.0)
        wo = jnp.exp(mo - mx)
        den = wo * lo_ref[...] + jnp.sum(w * ls_ref[...], axis=1, keepdims=True)
        num = wo * ao_ref[...]
        for jj in range(nbk):
            num = num + w[:, jj:jj + 1] * acc_ref[jj]
        res = num / den
        hl = lax.broadcasted_iota(jnp.int32, (1, da), 1) // hd
        out = jnp.zeros((dec, da), F32)
        for h in range(nh):
            out = jnp.where(hl == h, res[h * dec:(h + 1) * dec, :], out)
        o_ref[...] = out


def _attn_sample(slopes, page_table, q, k, v, cache_kt, cache_vt, layer, tp, bs, dec):
    t, da = q.shape
    nh = da // HEAD_DIM
    n_pages = page_table.shape[1]
    past_len = n_pages * PAGE_SIZE
    blk = MOBA_BLOCK
    assert blk == 2 * PAGE_SIZE and past_len % blk == 0 and tp % dec == 0
    nbk = past_len // blk
    nb_total = -(-(past_len + dec) // blk)
    assert nb_total == nbk + 1 and nbk <= LANES
    nr = nh * dec
    base = tp // dec
    new_rows = lambda b, j, pt: (base + b, 0)
    page = lambda o: (lambda b, j, pt: (layer, pt[b, 2 * j + o], 0, 0, 0))
    page_spec = lambda o: pl.BlockSpec((None, None, nh, HEAD_DIM, PAGE_SIZE), page(o))
    grid_spec = pltpu.PrefetchScalarGridSpec(
        num_scalar_prefetch=1,
        grid=(bs, nbk),
        in_specs=[pl.BlockSpec(memory_space=pltpu.SMEM)] + [pl.BlockSpec((dec, da), new_rows)] * 3
                 + [page_spec(0), page_spec(1), page_spec(0), page_spec(1)],
        out_specs=pl.BlockSpec((dec, da), lambda b, j, pt: (b, 0)),
        scratch_shapes=[pltpu.VMEM((2 * nr, da), BF16)] + [pltpu.VMEM((nr, 1), F32)] * 2
                       + [pltpu.VMEM((nr, da), F32)] + [pltpu.VMEM((nr, LANES), F32)] * 3
                       + [pltpu.VMEM((nbk, nr, da), F32)],
    )
    return pl.pallas_call(
        functools.partial(_attn_sample_body, nh=nh, hd=HEAD_DIM, dec=dec, nbk=nbk, blk=blk,
                          n_sel=min(MOBA_TOPK, nb_total), past_len=past_len),
        grid_spec=grid_spec,
        out_shape=jax.ShapeDtypeStruct((bs * dec, da), F32),
        compiler_params=_params(2),
        name="attn_sample",
    )(page_table, slopes, q, k, v, cache_kt, cache_kt, cache_vt, cache_vt)


def _merge_body(x_ref, cp_ref, cs_ref, ap_ref, as_ref, wg_ref, bg_ref, wco_ref, wao_ref, wo_ref, g1_ref, b1_ref, rw_ref, rb_ref,
                x1_ref, ridx_ref, rgate_ref, *, alpha, n_exp, ts, d, n_pt):
    is_prompt = pl.program_id(0) < n_pt
    c_pre = jnp.where(is_prompt, cp_ref[...], cs_ref[...])
    a_pre = jnp.where(is_prompt, ap_ref[...], as_ref[...])
    x = x_ref[...]
    xb = x.astype(BF16)
    gc = jax.nn.sigmoid(_bdot(xb, wg_ref[:, 0:d]) + bg_ref[:, 0:d])
    ga = jax.nn.sigmoid(_bdot(xb, wg_ref[:, d:2 * d]) + bg_ref[:, d:2 * d])
    c = _bdot(c_pre.astype(BF16), wco_ref[...])
    a = _bdot(a_pre.astype(BF16), wao_ref[...])
    merged = gc * c + ga * a
    m = _bdot(merged.astype(BF16), wo_ref[...])
    x1 = _layer_norm(alpha * x + m, g1_ref[...], b1_ref[...])
    nch = d // LANES
    for ch in range(nch):
        x1_ref[pl.ds(ch, ts, stride=nch), :] = x1[:, ch * LANES:(ch + 1) * LANES]

    xh, xl = _split_bf16(x1)
    tot = _bdot(xh, rw_ref[...]) + _bdot(xl, rw_ref[...])
    lane = lax.broadcasted_iota(jnp.int32, (1, LANES), 1)
    logits = tot + pltpu.roll(tot, LANES - n_exp, axis=1) + rb_ref[...]
    logits = jnp.where(lane < n_exp, logits, -jnp.inf)
    vals, idxs = [], []
    for _ in range(TOP_K):
        mx = jnp.max(logits, axis=1, keepdims=True)
        ix = jnp.min(jnp.where(logits == mx, lane, LANES), axis=1, keepdims=True)
        vals.append(mx)
        idxs.append(ix)
        logits = jnp.where(lane == ix, -jnp.inf, logits)
    es = [jnp.exp(v - vals[0]) for v in vals]
    den = es[0]
    for e in es[1:]:
        den = den + e
    ridx = jnp.zeros((ts, LANES), jnp.int32)
    rgate = jnp.zeros((ts, LANES), F32)
    for kk in range(TOP_K):
        ridx = jnp.where(lane == kk, idxs[kk], ridx)
        rgate = jnp.where(lane == kk, es[kk] / den, rgate)
    ridx_ref[...] = ridx
    rgate_ref[...] = rgate


def _merge(x, c_p, c_s, a_p, a_s, wg, bg, wco, wao, wo, g1, b1, rw, rb, ts, alpha, n_exp):
    t, d = x.shape
    dc = c_p.shape[1]
    da = a_p.shape[1]
    n_pt = c_p.shape[0] // ts
    prow = lambda i: (jnp.minimum(i, n_pt - 1), 0)
    srow = lambda i: (jnp.maximum(i - n_pt, 0), 0)
    nch = d // LANES
    row = lambda i: (i, 0)
    fixed = lambda i: (0, 0)
    full = lambda a: pl.BlockSpec(a.shape, fixed)
    return pl.pallas_call(
        functools.partial(_merge_body, alpha=alpha, n_exp=n_exp, ts=ts, d=d, n_pt=n_pt),
        grid=(t // ts,),
        in_specs=[pl.BlockSpec((ts, d), row), pl.BlockSpec((ts, dc), prow), pl.BlockSpec((ts, dc), srow),
                  pl.BlockSpec((ts, da), prow), pl.BlockSpec((ts, da), srow)]
                 + [full(a) for a in (wg, bg, wco, wao, wo, g1, b1, rw, rb)],
        out_specs=[pl.BlockSpec((ts * nch, LANES), row), pl.BlockSpec((ts, LANES), row), pl.BlockSpec((ts, LANES), row)],
        out_shape=[jax.ShapeDtypeStruct((t * nch, LANES), F32), jax.ShapeDtypeStruct((t, LANES), jnp.int32),
                   jax.ShapeDtypeStruct((t, LANES), F32)],
        compiler_params=_params(1),
        name="merge_router",
    )(x, c_p, c_s, a_p, a_s, wg, bg, wco, wao, wo, g1, b1, rw, rb)


def _route(ridx, rgate, n_exp, tm):
    t = ridx.shape[0]
    p0 = t * TOP_K
    e_flat = ridx[:, :TOP_K].reshape(p0)
    g_flat = rgate[:, :TOP_K].reshape(p0)
    onehot = (e_flat[:, None] == jnp.arange(n_exp, dtype=jnp.int32)[None, :]).astype(jnp.int32)
    csum = jnp.cumsum(onehot, axis=0)
    rank = jnp.take_along_axis(csum, e_flat[:, None], axis=1)[:, 0] - 1
    counts = csum[-1]
    ntile_e = (counts + tm - 1) // tm
    tiles_end = jnp.cumsum(ntile_e)
    off = (tiles_end - ntile_e) * tm
    pos = off[e_flat] + rank
    n_used = tiles_end[-1]
    n_tiles = p0 // tm + n_exp
    tile_ids = jnp.arange(n_tiles, dtype=jnp.int32)
    te = jnp.searchsorted(tiles_end, jnp.minimum(tile_ids, n_used - 1), side="right").astype(jnp.int32)
    order = jnp.argsort(e_flat, stable=True).astype(jnp.int32)
    su = jnp.cumsum(counts) - counts
    r_ids = jnp.arange(n_tiles * tm, dtype=jnp.int32)
    e_r = te[r_ids // tm]
    r_in = r_ids - off[e_r]
    valid = (r_in < counts[e_r]) & (r_ids // tm < n_used)
    src = order[jnp.clip(su[e_r] + r_in, 0, p0 - 1)]
    rowtok = jnp.where(valid, src // TOP_K, 0).astype(jnp.int32)
    rowgate = jnp.where(valid, g_flat[src], 0.0).astype(F32)[:, None]
    return te, n_used.astype(jnp.int32).reshape(1), rowtok, rowgate, pos.astype(jnp.int32)


def _moe_body(te_ref, nused_ref, rowtok_ref, x_hbm, wgu_ref, bgu_ref, wdn_ref, bdn_ref, rg_ref, y_ref,
              xbuf0, xbuf1, sem, wgu_bf, wdn_bf, *, tm, d, de, nch):
    i = pl.program_id(0)
    n_used = nused_ref[0]
    bufs = (xbuf0, xbuf1)

    def row_copy(tok, r, slot):
        return pltpu.make_async_copy(
            x_hbm.at[pl.ds(pl.multiple_of(tok * nch, nch), nch), :],
            bufs[slot].at[pl.ds(pl.multiple_of(r * nch, nch), nch), :],
            sem.at[slot])

    def issue(tile, slot):
        def body(r, carry):
            row_copy(rowtok_ref[tile * tm + r], r, slot).start()
            return carry
        lax.fori_loop(0, tm, body, 0)

    def wait(slot):
        def body(r, carry):
            row_copy(0, r, slot).wait()
            return carry
        lax.fori_loop(0, tm, body, 0)

    @pl.when(i == 0)
    def _():
        issue(0, 0)

    changed = (i == 0) | (te_ref[i] != te_ref[jnp.maximum(i - 1, 0)])

    @pl.when(changed & (i < n_used))
    def _():
        rchunk = 64
        def body(c, carry):
            r0 = pl.multiple_of(c * rchunk, rchunk)
            wgu_bf[pl.ds(r0, rchunk), :] = wgu_ref[pl.ds(r0, rchunk), :].astype(BF16)
            return carry
        lax.fori_loop(0, d // rchunk, body, 0)
        def body2(c, carry):
            r0 = pl.multiple_of(c * rchunk, rchunk)
            wdn_bf[pl.ds(r0, rchunk), :] = wdn_ref[pl.ds(r0, rchunk), :].astype(BF16)
            return carry
        lax.fori_loop(0, de // rchunk, body2, 0)

    for slot in (0, 1):
        @pl.when((i % 2 == slot) & (i < n_used))
        def _(slot=slot):
            @pl.when(i + 1 < n_used)
            def _():
                issue(i + 1, 1 - slot)
            wait(slot)
            buf = bufs[slot]
            x = jnp.concatenate([buf[pl.ds(ch, tm, stride=nch), :] for ch in range(nch)], axis=1).astype(BF16)
            h = _bdot(x, wgu_bf[...]) + bgu_ref[...]
            gate = jnp.minimum(h[:, 0:de], SWIGLU_LIMIT)
            up = jnp.clip(h[:, de:2 * de], -SWIGLU_LIMIT, SWIGLU_LIMIT)
            act = (up + 1.0) * (gate * jax.nn.sigmoid(SWIGLU_ALPHA * gate))
            y = (_bdot(act.astype(BF16), wdn_bf[...]) + bdn_ref[...]) * rg_ref[...]
            for ch in range(nch):
                y_ref[pl.ds(ch, tm, stride=nch), :] = y[:, ch * LANES:(ch + 1) * LANES]

    @pl.when(i >= n_used)
    def _():
        y_ref[...] = jnp.zeros(y_ref.shape, F32)


def _moe(te, n_used, rowtok, rowgate, x1t, w_gu, b_gu, w_dn, b_dn, layer, tm, d):
    n_tiles = te.shape[0]
    de = w_dn.shape[2]
    nch = d // LANES
    wmap = lambda i, te, nu, rt: (layer, te[i], 0, 0)
    grid_spec = pltpu.PrefetchScalarGridSpec(
        num_scalar_prefetch=3,
        grid=(n_tiles,),
        in_specs=[pl.BlockSpec(memory_space=pl.ANY),
                  pl.BlockSpec((None, None, d, 2 * de), wmap), pl.BlockSpec((None, None, 1, 2 * de), wmap),
                  pl.BlockSpec((None, None, de, d), wmap), pl.BlockSpec((None, None, 1, d), wmap),
                  pl.BlockSpec((tm, 1), lambda i, te, nu, rt: (i, 0))],
        out_specs=pl.BlockSpec((tm * nch, LANES), lambda i, te, nu, rt: (i, 0)),
        scratch_shapes=[pltpu.VMEM((tm * nch, LANES), F32)] * 2 + [pltpu.SemaphoreType.DMA((2,))]
                       + [pltpu.VMEM((d, 2 * de), BF16), pltpu.VMEM((de, d), BF16)],
    )
    return pl.pallas_call(
        functools.partial(_moe_body, tm=tm, d=d, de=de, nch=nch),
        grid_spec=grid_spec,
        out_shape=jax.ShapeDtypeStruct((n_tiles * tm * nch, LANES), F32),
        compiler_params=_params(1),
        name="moe_experts",
    )(te, n_used, rowtok, x1t, w_gu, b_gu, w_dn, b_dn, rowgate)


def _combine_body(pos_ref, ys_hbm, x1_ref, p_ref, g2_ref, b2_ref, wpg_ref, bpg_ref, wp_ref, g3_ref, b3_ref, o_ref,
                  ybuf0, ybuf1, sem, *, tf, nch, alpha):
    i = pl.program_id(0)
    n = pl.num_programs(0)
    bufs = (ybuf0, ybuf1)
    npair = tf * TOP_K

    def row_copy(src_row, r, slot):
        tok = lax.div(r, TOP_K)
        kk = lax.rem(r, TOP_K)
        return pltpu.make_async_copy(
            ys_hbm.at[pl.ds(pl.multiple_of(src_row * nch, nch), nch), :],
            bufs[slot].at[pl.ds(pl.multiple_of((kk * tf + tok) * nch, nch), nch), :],
            sem.at[slot])

    def issue(tile, slot):
        def body(r, carry):
            row_copy(pos_ref[tile * npair + r], r, slot).start()
            return carry
        lax.fori_loop(0, npair, body, 0)

    def wait(slot):
        def body(r, carry):
            row_copy(0, r, slot).wait()
            return carry
        lax.fori_loop(0, npair, body, 0)

    @pl.when(i == 0)
    def _():
        issue(0, 0)

    for slot in (0, 1):
        @pl.when(i % 2 == slot)
        def _(slot=slot):
            @pl.when(i + 1 < n)
            def _():
                issue(i + 1, 1 - slot)
            wait(slot)
            buf = bufs[slot]
            fs, xs = [], []
            for ch in range(nch):
                f = buf[pl.ds(ch, tf, stride=nch), :]
                for kk in range(1, TOP_K):
                    f = f + buf[pl.ds(kk * tf * nch + ch, tf, stride=nch), :]
                fs.append(f)
                xs.append(x1_ref[pl.ds(ch, tf, stride=nch), :])
            f = jnp.concatenate(fs, axis=1)
            x1 = jnp.concatenate(xs, axis=1)
            x2 = _layer_norm(alpha * x1 + f, g2_ref[...], b2_ref[...])
            e = jax.nn.sigmoid(_bdot(x2.astype(BF16), wpg_ref[...]) + bpg_ref[...]) \
                * _bdot(p_ref[...].astype(BF16), wp_ref[...])
            o_ref[...] = _layer_norm(alpha * x2 + e, g3_ref[...], b3_ref[...])


def _combine(pos, ys, x1t, p, g2, b2, wpg, bpg, wp, g3, b3, tf, d, alpha):
    t = p.shape[0]
    nch = d // LANES
    row = lambda i, pos: (i, 0)
    fixed = lambda i, pos: (0, 0)
    full = lambda a: pl.BlockSpec(a.shape, fixed)
    grid_spec = pltpu.PrefetchScalarGridSpec(
        num_scalar_prefetch=1,
        grid=(t // tf,),
        in_specs=[pl.BlockSpec(memory_space=pl.ANY), pl.BlockSpec((tf * nch, LANES), row),
                  pl.BlockSpec((tf, p.shape[1]), row)] + [full(a) for a in (g2, b2, wpg, bpg, wp, g3, b3)],
        out_specs=pl.BlockSpec((tf, d), row),
        scratch_shapes=[pltpu.VMEM((TOP_K * tf * nch, LANES), F32)] * 2 + [pltpu.SemaphoreType.DMA((2,))],
    )
    return pl.pallas_call(
        functools.partial(_combine_body, tf=tf, nch=nch, alpha=alpha),
        grid_spec=grid_spec,
        out_shape=jax.ShapeDtypeStruct((t, d), F32),
        compiler_params=_params(1),
        name="combine_ple",
    )(pos, ys, x1t, p, g2, b2, wpg, bpg, wp, g3, b3)


def kernel(x_prompt, x_sample, p_prompt, p_sample, cache_k, cache_v, state_conv, page_table, w_in, b_in, conv_w, conv_b, conv_ln_g, conv_ln_b, w_conv_out, w_attn_out, w_out, ln1_g, ln1_b, router_w, router_b, exp_w_gu, exp_b_gu, exp_w_down, exp_b_down, ln2_g, ln2_b, ple_w, ple_gate_w, ple_gate_b, ln3_g, ln3_b):
    bp, sp, d = x_prompt.shape
    bs, dec, _ = x_sample.shape
    depth = w_in.shape[0]
    d_conv = conv_w.shape[2]
    d_attn = w_attn_out.shape[1]
    nh = d_attn // HEAD_DIM
    n_exp = router_w.shape[2]
    tp, tsmp = bp * sp, bs * dec
    alpha = (2 * depth) ** 0.25
    assert 2 * n_exp <= LANES and d % LANES == 0

    ts = _pick_tile(512, tp, tsmp)
    tf = _pick_tile(256, tp, tsmp)
    tm = 256
    row2 = lambda a: a.reshape(1, -1)

    x = jnp.concatenate([x_prompt.reshape(tp, d), x_sample.reshape(tsmp, d)], axis=0)
    slopes = jnp.asarray([2.0 ** (-8.0 * (h + 1) / nh) for h in range(nh)], dtype=F32)
    cache_kt = jnp.transpose(cache_k, (0, 1, 3, 4, 2))
    cache_vt = jnp.transpose(cache_v, (0, 1, 3, 4, 2))
    b_gu4 = exp_b_gu[:, :, None, :]
    b_dn4 = exp_b_down[:, :, None, :]

    k_p, v_p, c_p, k_s, v_s, c_s = [], [], [], [], [], []
    n_qkv = 2 * d_conv + 3 * d_attn
    for l in range(depth):
        p = jnp.concatenate([p_prompt[l].reshape(tp, -1), p_sample[l].reshape(tsmp, -1)], axis=0)
        u, q, k, v = _inproj(x, w_in[l][:, :n_qkv].astype(BF16), row2(b_in[l][:n_qkv]), ts, d_conv, d_attn)

        conv_args = (conv_w[l], row2(conv_b[l]), row2(conv_ln_g[l]), row2(conv_ln_b[l]))
        c_p_pre = _conv_prompt(u, *conv_args, bp, sp)
        u_t = jnp.transpose(u[tp:].reshape(bs, dec, d_conv), (1, 0, 2))
        c_s_t = _conv_sample(u_t, jnp.transpose(state_conv[l], (1, 0, 2)), *conv_args)
        c_s_pre = jnp.transpose(c_s_t, (1, 0, 2)).reshape(tsmp, d_conv)

        a_p_pre = _attn_prompt(slopes, q, k, v, bp, sp)
        a_s_pre = _attn_sample(slopes, page_table, q, k, v, cache_kt, cache_vt, l, tp, bs, dec)

        rw_hi, rw_lo = _split_bf16(router_w[l])
        rw = jnp.zeros((d, LANES), BF16).at[:, :n_exp].set(rw_hi).at[:, n_exp:2 * n_exp].set(rw_lo)
        rb = jnp.zeros((1, LANES), F32).at[0, :n_exp].set(router_b[l])
        x1t, ridx, rgate = _merge(
            x, c_p_pre, c_s_pre, a_p_pre, a_s_pre, w_in[l][:, n_qkv:].astype(BF16), row2(b_in[l][n_qkv:]),
            w_conv_out[l].astype(BF16), w_attn_out[l].astype(BF16), w_out[l].astype(BF16),
            row2(ln1_g[l]), row2(ln1_b[l]), rw, rb, ts, alpha, n_exp)

        te, n_used, rowtok, rowgate, pos = _route(ridx, rgate, n_exp, tm)
        ys = _moe(te, n_used, rowtok, rowgate, x1t, exp_w_gu, b_gu4, exp_w_down, b_dn4, l, tm, d)
        x = _combine(pos, ys, x1t, p, row2(ln2_g[l]), row2(ln2_b[l]), ple_gate_w[l].astype(BF16),
                     row2(ple_gate_b[l]), ple_w[l].astype(BF16), row2(ln3_g[l]), row2(ln3_b[l]), tf, d, alpha)

        u_p = u[:tp].reshape(bp, sp, d_conv)
        u_s = u[tp:].reshape(bs, dec, d_conv)
        hist = conv_w.shape[1] - 1
        k_p.append(k[:tp].reshape(bp, sp, nh, HEAD_DIM))
        v_p.append(v[:tp].reshape(bp, sp, nh, HEAD_DIM))
        c_p.append(u_p[:, sp - hist:, :])
        k_s.append(k[tp:].reshape(bs, dec, nh, HEAD_DIM))
        v_s.append(v[tp:].reshape(bs, dec, nh, HEAD_DIM))
        c_s.append(jnp.concatenate([state_conv[l], u_s], axis=1)[:, -hist:, :])

    return (x[:tp].reshape(bp, sp, d), x[tp:].reshape(bs, dec, d), jnp.stack(k_p), jnp.stack(v_p), jnp.stack(c_p),
            jnp.stack(k_s), jnp.stack(v_s), jnp.stack(c_s))
```

```python
import functools

import jax
import jax.numpy as jnp
from jax import lax
from jax.experimental import pallas as pl
from jax.experimental.pallas import tpu as pltpu

F32 = jnp.float32
BF16 = jnp.bfloat16

HEAD_DIM = 64
MOBA_BLOCK = 256
MOBA_TOPK = 3
PAGE_SIZE = 128
TOP_K = 4
SWIGLU_ALPHA = 1.702
SWIGLU_LIMIT = 7.0
LN_EPS = 1e-5

LANES = 128
SUBLANES = 8
VMEM_LIMIT_BYTES = 56 * 1024 * 1024
DMA_ISSUE_UNROLL = 8

NT_DIMS = (((1,), (1,)), ((), ()))


def _params(n_axes, vmem=None):
    return pltpu.CompilerParams(
        dimension_semantics=("arbitrary",) * n_axes,
        vmem_limit_bytes=VMEM_LIMIT_BYTES if vmem is None else vmem,
    )


def _pick_tile(cap, *sizes):
    t = cap
    while any(s % t for s in sizes):
        t //= 2
    return t


def _layer_norm(x, g, b):
    mu = jnp.mean(x, axis=-1, keepdims=True)
    xc = x - mu
    var = jnp.mean(xc * xc, axis=-1, keepdims=True)
    return xc * lax.rsqrt(var + LN_EPS) * g + b


def _split_bf16(a):
    hi = a.astype(BF16)
    lo = (a - hi.astype(F32)).astype(BF16)
    return hi, lo


def _dot3_nt(a, b):
    ah, al = _split_bf16(a)
    bh, bl = _split_bf16(b)
    f = lambda x, y: lax.dot_general(x, y, NT_DIMS, preferred_element_type=F32)
    return f(ah, bh) + (f(ah, bl) + f(al, bh))


def _bdot(a, b):
    return jnp.dot(a, b, preferred_element_type=F32)


def _split_maps(n_pt):
    return (lambda i: (jnp.minimum(i, n_pt - 1), 0)), (lambda i: (jnp.maximum(i - n_pt, 0), 0))


def _inproj_body(xp_ref, xs_ref, w_ref, b_ref, *out_refs, d_conv, d_attn, n_pt):
    is_prompt = pl.program_id(0) < n_pt
    x = jnp.where(is_prompt, xp_ref[...], xs_ref[...]).astype(BF16)

    def proj(lo, n):
        return _bdot(x, w_ref[:, lo:lo + n]) + b_ref[:, lo:lo + n]

    o = 2 * d_conv
    vals = (proj(0, d_conv) * jax.nn.sigmoid(proj(d_conv, d_conv)),
            proj(o, d_attn), proj(o + d_attn, d_attn), proj(o + 2 * d_attn, d_attn))

    @pl.when(is_prompt)
    def _():
        for ref, val in zip(out_refs[0:4], vals):
            ref[...] = val

    @pl.when(jnp.logical_not(is_prompt))
    def _():
        for ref, val in zip(out_refs[4:8], vals):
            ref[...] = val


def _inproj(x_p, x_s, w_bf, b, ts, d_conv, d_attn):
    tp, d = x_p.shape
    tsmp = x_s.shape[0]
    n = w_bf.shape[1]
    n_pt = tp // ts
    prow, srow = _split_maps(n_pt)
    fixed = lambda i: (0, 0)
    widths = (d_conv, d_attn, d_attn, d_attn)
    return pl.pallas_call(
        functools.partial(_inproj_body, d_conv=d_conv, d_attn=d_attn, n_pt=n_pt),
        grid=((tp + tsmp) // ts,),
        in_specs=[pl.BlockSpec((ts, d), prow), pl.BlockSpec((ts, d), srow), pl.BlockSpec((d, n), fixed),
                  pl.BlockSpec((1, n), fixed)],
        out_specs=[pl.BlockSpec((ts, w), prow) for w in widths] + [pl.BlockSpec((ts, w), srow) for w in widths],
        out_shape=[jax.ShapeDtypeStruct((tp, w), F32) for w in widths]
                  + [jax.ShapeDtypeStruct((tsmp, w), F32) for w in widths],
        compiler_params=_params(1),
        name="inproj",
    )(x_p, x_s, w_bf, b)


def _conv_prompt_body(u_ref, w_ref, cb_ref, g_ref, b_ref, o_ref, win_ref, *, ts, width, halo, rc):
    s = pl.program_id(1)
    c = u_ref.shape[1]

    @pl.when(s == 0)
    def _():
        win_ref[0:halo, :] = jnp.zeros((halo, c), F32)

    @pl.when(s > 0)
    def _():
        win_ref[0:halo, :] = win_ref[ts:ts + halo, :]

    win_ref[halo:halo + ts, :] = u_ref[...]
    off = halo - (width - 1)
    for r0 in range(0, ts, rc):
        acc = jnp.broadcast_to(cb_ref[...], (rc, c))
        for j in range(width):
            acc = acc + win_ref[pl.ds(r0 + off + j, rc), :] * w_ref[j:j + 1, :]
        y = _layer_norm(acc, g_ref[...], b_ref[...])
        o_ref[r0:r0 + rc, :] = y * jax.nn.sigmoid(y)


def _conv_prompt(u, conv_w, conv_b, ln_g, ln_b, bp, sp):
    c = u.shape[1]
    width = conv_w.shape[0]
    halo = -(-(width - 1) // SUBLANES) * SUBLANES
    ts = _pick_tile(256, sp)
    assert ts >= halo
    ns = sp // ts
    tile = lambda b, s: (b * ns + s, 0)
    fixed = lambda b, s: (0, 0)
    return pl.pallas_call(
        functools.partial(_conv_prompt_body, ts=ts, width=width, halo=halo, rc=min(64, ts)),
        grid=(bp, ns),
        in_specs=[pl.BlockSpec((ts, c), tile), pl.BlockSpec((width, c), fixed)] + [pl.BlockSpec((1, c), fixed)] * 3,
        out_specs=pl.BlockSpec((ts, c), tile),
        out_shape=jax.ShapeDtypeStruct((bp * sp, c), F32),
        scratch_shapes=[pltpu.VMEM((halo + ts, c), F32)],
        compiler_params=_params(2),
        name="conv_prompt",
    )(u, conv_w, conv_b, ln_g, ln_b)


def _conv_sample_body(u_ref, h_ref, w_ref, cb_ref, g_ref, b_ref, o_ref, *, nb, dec, width):
    c = u_ref.shape[2]
    hist = width - 1

    def full_row(r):
        return h_ref[r] if r < hist else u_ref[r - hist]

    for t in range(dec):
        acc = jnp.broadcast_to(cb_ref[...], (nb, c))
        for j in range(width):
            acc = acc + full_row(t + j) * w_ref[j:j + 1, :]
        y = _layer_norm(acc, g_ref[...], b_ref[...])
        o_ref[t] = y * jax.nn.sigmoid(y)


def _conv_sample(u_t, hist_t, conv_w, conv_b, ln_g, ln_b):
    dec, bs, c = u_t.shape
    width = conv_w.shape[0]
    nb = min(32, bs)
    assert bs % nb == 0
    blockg = lambda g: (0, g, 0)
    fixed = lambda g: (0, 0)
    return pl.pallas_call(
        functools.partial(_conv_sample_body, nb=nb, dec=dec, width=width),
        grid=(bs // nb,),
        in_specs=[pl.BlockSpec((dec, nb, c), blockg), pl.BlockSpec((width - 1, nb, c), blockg),
                  pl.BlockSpec((width, c), fixed)] + [pl.BlockSpec((1, c), fixed)] * 3,
        out_specs=pl.BlockSpec((dec, nb, c), blockg),
        out_shape=jax.ShapeDtypeStruct((dec, bs, c), F32),
        compiler_params=_params(1),
        name="conv_sample",
    )(u_t, hist_t, conv_w, conv_b, ln_g, ln_b)


def _attn_prompt_body(slopes_ref, q_ref, k_ref, v_ref, o_ref, kmean_ref, *, blk, nb, n_sel, hd):
    hp = pl.program_id(1)
    qi = pl.program_id(2)

    @pl.when(qi == 0)
    def _():
        kmean_ref[...] = jnp.zeros(kmean_ref.shape, F32)
        for j in range(nb):
            kmean_ref[j:j + 1, :] = jnp.mean(k_ref[j * blk:(j + 1) * blk, :], axis=0, keepdims=True)

    nhl = LANES // hd
    rows = nhl * blk
    lane = lax.broadcasted_iota(jnp.int32, (1, LANES), 1)
    rowi = lax.broadcasted_iota(jnp.int32, (rows, 1), 0)
    rhead = rowi // blk
    rq = rowi % blk
    col = lax.broadcasted_iota(jnp.int32, (1, blk), 1)
    scale = HEAD_DIM ** -0.5
    neg_inf = -jnp.inf
    slope = jnp.zeros((rows, 1), F32)
    for hh in range(nhl):
        slope = jnp.where(rhead == hh, slopes_ref[hp * nhl + hh], slope)
    q = q_ref[...]
    q2 = jnp.where(lane // hd == rhead, jnp.concatenate([q] * nhl, axis=0), 0.0)
    q2b = q2.astype(BF16)

    gate = _dot3_nt(q2, kmean_ref[...])
    rank = jnp.zeros((rows, LANES), F32)
    for jj in range(nb):
        gj = gate[:, jj:jj + 1]
        beats = (jj < qi) & ((gj > gate) | ((gj == gate) & (jj < lane)))
        rank = rank + beats.astype(F32)
    sel = jnp.where((rank < n_sel) & (lane < qi), 1.0, 0.0)

    def scores(j):
        kb = k_ref[pl.ds(pl.multiple_of(j * blk, blk), blk), :].astype(BF16)
        s = lax.dot_general(q2b, kb, NT_DIMS, preferred_element_type=F32) * scale
        dist = ((qi - j) * blk + rq - col).astype(F32)
        return s - slope * dist, dist

    def pv(p, j):
        vb = v_ref[pl.ds(pl.multiple_of(j * blk, blk), blk), :].astype(BF16)
        return _bdot(p.astype(BF16), vb)

    s, dist = scores(qi)
    s = jnp.where(dist >= 0, s, neg_inf)
    m = jnp.max(s, axis=1, keepdims=True)
    p = jnp.exp(s - m)
    l = jnp.sum(p, axis=1, keepdims=True)
    acc = pv(p, qi)

    def body(j, carry):
        m, l, acc = carry
        sj = jnp.sum(jnp.where(lane == j, sel, 0.0), axis=1, keepdims=True)
        s, _ = scores(j)
        s = jnp.where(sj > 0.5, s, neg_inf)
        m_new = jnp.maximum(m, jnp.max(s, axis=1, keepdims=True))
        a = jnp.exp(m - m_new)
        p = jnp.exp(s - m_new)
        return m_new, a * l + jnp.sum(p, axis=1, keepdims=True), a * acc + pv(p, j)

    m, l, acc = lax.fori_loop(0, qi, body, (m, l, acc))
    res = acc / l
    o = res[0:blk, :]
    for hh in range(1, nhl):
        o = jnp.where(lane // hd == hh, res[hh * blk:(hh + 1) * blk, :], o)
    o_ref[...] = o


def _attn_prompt(slopes, q, k, v, bp, sp):
    da = q.shape[1]
    blk = MOBA_BLOCK
    nb = sp // blk
    assert sp % blk == 0 and nb <= LANES and da % LANES == 0
    qtile = lambda b, hp, qi: (b * nb + qi, hp)
    seq = lambda b, hp, qi: (b, hp)
    return pl.pallas_call(
        functools.partial(_attn_prompt_body, blk=blk, nb=nb, n_sel=min(MOBA_TOPK, nb), hd=HEAD_DIM),
        grid=(bp, da // LANES, nb),
        in_specs=[pl.BlockSpec(memory_space=pltpu.SMEM), pl.BlockSpec((blk, LANES), qtile),
                  pl.BlockSpec((sp, LANES), seq), pl.BlockSpec((sp, LANES), seq)],
        out_specs=pl.BlockSpec((blk, LANES), qtile),
        out_shape=jax.ShapeDtypeStruct((bp * sp, da), F32),
        scratch_shapes=[pltpu.VMEM((LANES, LANES), F32)],
        compiler_params=_params(3),
        name="attn_prompt",
    )(slopes, q, k, v)


def _attn_sample_body(pt_ref, slopes_ref, q_ref, kn_ref, vn_ref, *refs, nh, hd, dec, nbk, n_pages, blk, n_sel, past_len):
    del pt_ref
    k_refs = refs[0:n_pages]
    v_refs = refs[n_pages:2 * n_pages]
    o_ref = refs[2 * n_pages]
    acc_ref = refs[2 * n_pages + 1]
    da = nh * hd
    nr = nh * dec
    ppb = blk // PAGE_SIZE
    scale = HEAD_DIM ** -0.5
    lane = lax.broadcasted_iota(jnp.int32, (1, LANES), 1)
    rowi = lax.broadcasted_iota(jnp.int32, (nr, 1), 0)
    slope = jnp.zeros((nr, 1), F32)
    for h in range(nh):
        slope = jnp.where(rowi // dec == h, slopes_ref[h], slope)
    tq = rowi % dec

    q = q_ref[...]
    hl = lax.broadcasted_iota(jnp.int32, (1, da), 1) // hd
    qbd = jnp.where(rowi // dec == hl, jnp.concatenate([q] * nh, axis=0), 0.0)
    qh, ql = _split_bf16(qbd)
    q2 = jnp.concatenate([qh, ql], axis=0)

    cols = [jnp.sum(qbd * kn_ref[t:t + 1, :], axis=1, keepdims=True) for t in range(dec)]
    s = jnp.concatenate(cols, axis=1) * scale
    tk = lax.broadcasted_iota(jnp.int32, (1, dec), 1)
    dist = (tq - tk).astype(F32)
    s = jnp.where(dist >= 0, s - slope * dist, -jnp.inf)
    mo = jnp.max(s, axis=1, keepdims=True)
    p = jnp.exp(s - mo)
    lo = jnp.sum(p, axis=1, keepdims=True)
    ao = jnp.zeros((nr, da), F32)
    for t in range(dec):
        ao = ao + p[:, t:t + 1] * vn_ref[t:t + 1, :]

    g = jnp.zeros((nr, LANES), F32)
    ms = jnp.zeros((nr, LANES), F32)
    ls = jnp.zeros((nr, LANES), F32)
    for j in range(nbk):
        kt = jnp.concatenate([k_refs[ppb * j + o][...].reshape(da, PAGE_SIZE) for o in range(ppb)], axis=1).astype(BF16)
        vt = jnp.concatenate([v_refs[ppb * j + o][...].reshape(da, PAGE_SIZE) for o in range(ppb)], axis=1).astype(BF16)
        s2 = _bdot(q2, kt)
        raw = s2[0:nr, :] + s2[nr:2 * nr, :]
        gate = jnp.mean(raw, axis=1, keepdims=True)
        kpos = j * blk + lax.broadcasted_iota(jnp.int32, (1, blk), 1)
        dist = (past_len + tq - kpos).astype(F32)
        s = raw * scale - slope * dist
        m = jnp.max(s, axis=1, keepdims=True)
        p = jnp.exp(s - m)
        acc_ref[j] = lax.dot_general(p.astype(BF16), vt, NT_DIMS, preferred_element_type=F32)
        g = jnp.where(lane == j, gate, g)
        ms = jnp.where(lane == j, m, ms)
        ls = jnp.where(lane == j, jnp.sum(p, axis=1, keepdims=True), ls)

    rank = jnp.zeros((nr, LANES), F32)
    for jj in range(nbk):
        gj = g[:, jj:jj + 1]
        rank = rank + ((gj > g) | ((gj == g) & (jj < lane))).astype(F32)
    sel = (rank < n_sel) & (lane < nbk)
    mx = jnp.maximum(mo, jnp.max(jnp.where(sel, ms, -jnp.inf), axis=1, keepdims=True))
    w = jnp.where(sel, jnp.exp(jnp.where(sel, ms, mx) - mx), 0.0)
    wo = jnp.exp(mo - mx)
    den = wo * lo + jnp.sum(w * ls, axis=1, keepdims=True)
    num = wo * ao
    for jj in range(nbk):
        num = num + w[:, jj:jj + 1] * acc_ref[jj]
    res = num / den
    out = jnp.zeros((dec, da), F32)
    for h in range(nh):
        out = jnp.where(hl == h, res[h * dec:(h + 1) * dec, :], out)
    o_ref[...] = out


def _attn_sample(slopes, page_table, q, k, v, cache_kt, cache_vt, layer, bs, dec):
    da = q.shape[1]
    nh = da // HEAD_DIM
    n_pages = page_table.shape[1]
    past_len = n_pages * PAGE_SIZE
    blk = MOBA_BLOCK
    assert blk % PAGE_SIZE == 0 and past_len % blk == 0
    nbk = past_len // blk
    nb_total = -(-(past_len + dec) // blk)
    assert nb_total == nbk + 1 and nbk <= LANES
    nr = nh * dec
    new_rows = lambda b, pt: (b, 0)
    page_spec = lambda pg: pl.BlockSpec((None, None, nh, HEAD_DIM, PAGE_SIZE),
                                        lambda b, pt: (layer, pt[b, pg], 0, 0, 0))
    grid_spec = pltpu.PrefetchScalarGridSpec(
        num_scalar_prefetch=1,
        grid=(bs,),
        in_specs=[pl.BlockSpec(memory_space=pltpu.SMEM)] + [pl.BlockSpec((dec, da), new_rows)] * 3
                 + [page_spec(pg) for pg in range(n_pages)] * 2,
        out_specs=pl.BlockSpec((dec, da), lambda b, pt: (b, 0)),
        scratch_shapes=[pltpu.VMEM((nbk, nr, da), F32)],
    )
    return pl.pallas_call(
        functools.partial(_attn_sample_body, nh=nh, hd=HEAD_DIM, dec=dec, nbk=nbk, n_pages=n_pages, blk=blk,
                          n_sel=min(MOBA_TOPK, nb_total), past_len=past_len),
        grid_spec=grid_spec,
        out_shape=jax.ShapeDtypeStruct((bs * dec, da), F32),
        compiler_params=_params(1),
        name="attn_sample",
    )(page_table, slopes, q, k, v, *([cache_kt] * n_pages), *([cache_vt] * n_pages))


def _merge_body(xp_ref, xs_ref, cp_ref, cs_ref, ap_ref, as_ref, wg_ref, bg_ref, wco_ref, wao_ref, wo_ref, g1_ref, b1_ref,
                rw_ref, rb_ref, x1_ref, ridx_ref, rgate_ref, *, alpha, n_exp, ts, d, n_pt):
    is_prompt = pl.program_id(0) < n_pt
    c_pre = jnp.where(is_prompt, cp_ref[...], cs_ref[...])
    a_pre = jnp.where(is_prompt, ap_ref[...], as_ref[...])
    x = jnp.where(is_prompt, xp_ref[...], xs_ref[...])
    xb = x.astype(BF16)
    gc = jax.nn.sigmoid(_bdot(xb, wg_ref[:, 0:d]) + bg_ref[:, 0:d])
    ga = jax.nn.sigmoid(_bdot(xb, wg_ref[:, d:2 * d]) + bg_ref[:, d:2 * d])
    c = _bdot(c_pre.astype(BF16), wco_ref[...])
    a = _bdot(a_pre.astype(BF16), wao_ref[...])
    merged = gc * c + ga * a
    m = _bdot(merged.astype(BF16), wo_ref[...])
    x1 = _layer_norm(alpha * x + m, g1_ref[...], b1_ref[...])
    nch = d // LANES
    for ch in range(nch):
        x1_ref[pl.ds(ch, ts, stride=nch), :] = x1[:, ch * LANES:(ch + 1) * LANES]

    xh, xl = _split_bf16(x1)
    tot = _bdot(xh, rw_ref[...]) + _bdot(xl, rw_ref[...])
    lane = lax.broadcasted_iota(jnp.int32, (1, LANES), 1)
    logits = tot + pltpu.roll(tot, LANES - n_exp, axis=1) + rb_ref[...]
    logits = jnp.where(lane < n_exp, logits, -jnp.inf)
    vals, idxs = [], []
    for _ in range(TOP_K):
        mx = jnp.max(logits, axis=1, keepdims=True)
        ix = jnp.min(jnp.where(logits == mx, lane, LANES), axis=1, keepdims=True)
        vals.append(mx)
        idxs.append(ix)
        logits = jnp.where(lane == ix, -jnp.inf, logits)
    es = [jnp.exp(v - vals[0]) for v in vals]
    den = es[0]
    for e in es[1:]:
        den = den + e
    ridx = jnp.zeros((ts, LANES), jnp.int32)
    rgate = jnp.zeros((ts, LANES), F32)
    for kk in range(TOP_K):
        ridx = jnp.where(lane == kk, idxs[kk], ridx)
        rgate = jnp.where(lane == kk, es[kk] / den, rgate)
    ridx_ref[...] = ridx
    rgate_ref[...] = rgate


def _merge(x_p, x_s, c_p, c_s, a_p, a_s, wg, bg, wco, wao, wo, g1, b1, rw, rb, ts, alpha, n_exp):
    d = x_p.shape[1]
    t = x_p.shape[0] + x_s.shape[0]
    dc = c_p.shape[1]
    da = a_p.shape[1]
    n_pt = x_p.shape[0] // ts
    prow, srow = _split_maps(n_pt)
    nch = d // LANES
    row = lambda i: (i, 0)
    fixed = lambda i: (0, 0)
    full = lambda a: pl.BlockSpec(a.shape, fixed)
    return pl.pallas_call(
        functools.partial(_merge_body, alpha=alpha, n_exp=n_exp, ts=ts, d=d, n_pt=n_pt),
        grid=(t // ts,),
        in_specs=[pl.BlockSpec((ts, d), prow), pl.BlockSpec((ts, d), srow),
                  pl.BlockSpec((ts, dc), prow), pl.BlockSpec((ts, dc), srow),
                  pl.BlockSpec((ts, da), prow), pl.BlockSpec((ts, da), srow)]
                 + [full(a) for a in (wg, bg, wco, wao, wo, g1, b1, rw, rb)],
        out_specs=[pl.BlockSpec((ts * nch, LANES), row), pl.BlockSpec((ts, LANES), row), pl.BlockSpec((ts, LANES), row)],
        out_shape=[jax.ShapeDtypeStruct((t * nch, LANES), F32), jax.ShapeDtypeStruct((t, LANES), jnp.int32),
                   jax.ShapeDtypeStruct((t, LANES), F32)],
        compiler_params=_params(1),
        name="merge_router",
    )(x_p, x_s, c_p, c_s, a_p, a_s, wg, bg, wco, wao, wo, g1, b1, rw, rb)


def _route(ridx, n_exp, tm):
    t = ridx.shape[0]
    p0 = t * TOP_K
    assert p0 % tm == 0
    n_tiles = p0 // tm
    n_items = n_tiles + n_exp - 1
    e_flat = ridx[:, :TOP_K].reshape(p0)
    order = jnp.argsort(e_flat, stable=True).astype(jnp.int32)
    counts = jnp.sum((e_flat[:, None] == jnp.arange(n_exp, dtype=jnp.int32)[None, :]).astype(jnp.int32), axis=0)
    end = jnp.cumsum(counts)
    start = end - counts
    n_items_e = jnp.where(counts > 0, (end - 1) // tm - start // tm + 1, 0)
    item_end = jnp.cumsum(n_items_e)
    item_start = item_end - n_items_e
    total = item_end[-1]
    ids = jnp.arange(n_items, dtype=jnp.int32)
    idc = jnp.minimum(ids, total - 1)
    it_exp = jnp.searchsorted(item_end, idc, side="right").astype(jnp.int32)
    it_tile = start[it_exp] // tm + (idc - item_start[it_exp])
    lo = jnp.maximum(start[it_exp], it_tile * tm) - it_tile * tm
    hi = jnp.minimum(end[it_exp], (it_tile + 1) * tm) - it_tile * tm
    valid = ids < total
    i32 = lambda a: a.astype(jnp.int32)
    return (i32(it_tile), it_exp, i32(jnp.where(valid, lo, 0)), i32(jnp.where(valid, hi, 0)),
            i32(total).reshape(1), order)


def _moe_body(tile_ref, exp_ref, lo_ref, hi_ref, total_ref, order_ref, x_hbm, wgu_ref, bgu_ref, wdn_ref, bdn_ref, ys_hbm,
              xbuf0, xbuf1, ybuf0, ybuf1, gsem, ssem, wgu_bf, wdn_bf, *, tm, d, de, nch, n_tiles, n_items):
    i = pl.program_id(0)
    total = total_ref[0]
    valid = i < total
    tile = tile_ref[i]
    prev = jnp.maximum(i - 1, 0)
    nxt = jnp.minimum(i + 1, n_items - 1)
    first = (i == 0) | (tile != tile_ref[prev])
    last = (i == total - 1) | (tile != tile_ref[nxt])
    new_expert = (i == 0) | (exp_ref[i] != exp_ref[prev])
    xbufs = (xbuf0, xbuf1)
    ybufs = (ybuf0, ybuf1)

    def gather_copy(pair, r, slot):
        tok = lax.div(pair, TOP_K)
        return pltpu.make_async_copy(
            x_hbm.at[pl.ds(pl.multiple_of(tok * nch, nch), nch), :],
            xbufs[slot].at[pl.ds(pl.multiple_of(r * nch, nch), nch), :],
            gsem.at[slot])

    def scatter_copy(pair, r, slot):
        return pltpu.make_async_copy(
            ybufs[slot].at[pl.ds(pl.multiple_of(r * nch, nch), nch), :],
            ys_hbm.at[pl.ds(pl.multiple_of(pair * nch, nch), nch), :],
            ssem.at[slot])

    def issue(copy, t, slot):
        def body(g, carry):
            for u in range(DMA_ISSUE_UNROLL):
                r = g * DMA_ISSUE_UNROLL + u
                copy(order_ref[t * tm + r], r, slot).start()
            return carry
        lax.fori_loop(0, tm // DMA_ISSUE_UNROLL, body, 0)

    def wait_all(copy, slot):
        if copy is gather_copy:
            whole = pltpu.make_async_copy(x_hbm.at[pl.ds(0, tm * nch), :], xbufs[slot], gsem.at[slot])
        else:
            whole = pltpu.make_async_copy(ybufs[slot], ys_hbm.at[pl.ds(0, tm * nch), :], ssem.at[slot])
        whole.wait()

    @pl.when(i == 0)
    def _():
        issue(gather_copy, 0, 0)

    @pl.when(valid & new_expert)
    def _():
        rchunk = 64
        def body(c, carry):
            r0 = pl.multiple_of(c * rchunk, rchunk)
            wgu_bf[pl.ds(r0, rchunk), :] = wgu_ref[pl.ds(r0, rchunk), :].astype(BF16)
            return carry
        lax.fori_loop(0, d // rchunk, body, 0)
        def body2(c, carry):
            r0 = pl.multiple_of(c * rchunk, rchunk)
            wdn_bf[pl.ds(r0, rchunk), :] = wdn_ref[pl.ds(r0, rchunk), :].astype(BF16)
            return carry
        lax.fori_loop(0, de // rchunk, body2, 0)

    for slot in (0, 1):
        @pl.when(valid & (tile % 2 == slot))
        def _(slot=slot):
            @pl.when(first)
            def _():
                @pl.when(tile + 1 < n_tiles)
                def _():
                    issue(gather_copy, tile + 1, 1 - slot)
                wait_all(gather_copy, slot)
                @pl.when(tile >= 2)
                def _():
                    wait_all(scatter_copy, slot)

            xbuf, ybuf = xbufs[slot], ybufs[slot]
            x = jnp.concatenate([xbuf[pl.ds(ch, tm, stride=nch), :] for ch in range(nch)], axis=1).astype(BF16)
            h = _bdot(x, wgu_bf[...]) + bgu_ref[...]
            gate = jnp.minimum(h[:, 0:de], SWIGLU_LIMIT)
            up = jnp.clip(h[:, de:2 * de], -SWIGLU_LIMIT, SWIGLU_LIMIT)
            act = (up + 1.0) * (gate * jax.nn.sigmoid(SWIGLU_ALPHA * gate))
            y = _bdot(act.astype(BF16), wdn_bf[...]) + bdn_ref[...]
            rows = lax.broadcasted_iota(jnp.int32, (tm, 1), 0)
            y = jnp.where((rows >= lo_ref[i]) & (rows < hi_ref[i]), y, 0.0)

            @pl.when(first)
            def _():
                for ch in range(nch):
                    ybuf[pl.ds(ch, tm, stride=nch), :] = y[:, ch * LANES:(ch + 1) * LANES]

            @pl.when(jnp.logical_not(first))
            def _():
                for ch in range(nch):
                    ybuf[pl.ds(ch, tm, stride=nch), :] = ybuf[pl.ds(ch, tm, stride=nch), :] + y[:, ch * LANES:(ch + 1) * LANES]

            @pl.when(last)
            def _():
                issue(scatter_copy, tile, slot)

    @pl.when(i == total - 1)
    def _():
        for tl in range(max(n_tiles - 2, 0), n_tiles):
            wait_all(scatter_copy, tl % 2)


def _moe(it_tile, it_exp, it_lo, it_hi, total, order, x1t, w_gu, b_gu, w_dn, b_dn, layer, tm, d):
    n_items = it_tile.shape[0]
    p0 = order.shape[0]
    n_tiles = p0 // tm
    de = w_dn.shape[2]
    nch = d // LANES
    wmap = lambda i, tl, ex, lo, hi, tot, od: (layer, ex[i], 0, 0)
    grid_spec = pltpu.PrefetchScalarGridSpec(
        num_scalar_prefetch=6,
        grid=(n_items,),
        in_specs=[pl.BlockSpec(memory_space=pl.ANY),
                  pl.BlockSpec((None, None, d, 2 * de), wmap), pl.BlockSpec((None, None, 1, 2 * de), wmap),
                  pl.BlockSpec((None, None, de, d), wmap), pl.BlockSpec((None, None, 1, d), wmap)],
        out_specs=pl.BlockSpec(memory_space=pl.ANY),
        scratch_shapes=[pltpu.VMEM((tm * nch, LANES), F32)] * 4 + [pltpu.SemaphoreType.DMA((2,))] * 2
                       + [pltpu.VMEM((d, 2 * de), BF16), pltpu.VMEM((de, d), BF16)],
    )
    return pl.pallas_call(
        functools.partial(_moe_body, tm=tm, d=d, de=de, nch=nch, n_tiles=n_tiles, n_items=n_items),
        grid_spec=grid_spec,
        out_shape=jax.ShapeDtypeStruct((p0 * nch, LANES), F32),
        compiler_params=_params(1),
        name="moe_experts",
    )(it_tile, it_exp, it_lo, it_hi, total, order, x1t, w_gu, b_gu, w_dn, b_dn)


def _combine_body(ys_ref, rg_ref, x1_ref, pp_ref, ps_ref, g2_ref, b2_ref, wpg_ref, bpg_ref, wp_ref, g3_ref, b3_ref,
                  op_ref, os_ref, *, tf, nch, alpha, n_pt):
    is_prompt = pl.program_id(0) < n_pt
    ple = jnp.where(is_prompt, pp_ref[...], ps_ref[...])
    rg = rg_ref[...]
    gk = [jnp.broadcast_to(rg[:, kk:kk + 1], (tf, LANES)) for kk in range(TOP_K)]
    stride = TOP_K * nch
    fs, xs = [], []
    for ch in range(nch):
        f = gk[0] * ys_ref[pl.ds(ch, tf, stride=stride), :]
        for kk in range(1, TOP_K):
            f = f + gk[kk] * ys_ref[pl.ds(kk * nch + ch, tf, stride=stride), :]
        fs.append(f)
        xs.append(x1_ref[pl.ds(ch, tf, stride=nch), :])
    f = jnp.concatenate(fs, axis=1)
    x1 = jnp.concatenate(xs, axis=1)
    x2 = _layer_norm(alpha * x1 + f, g2_ref[...], b2_ref[...])
    e = jax.nn.sigmoid(_bdot(x2.astype(BF16), wpg_ref[...]) + bpg_ref[...]) * _bdot(ple.astype(BF16), wp_ref[...])
    x3 = _layer_norm(alpha * x2 + e, g3_ref[...], b3_ref[...])

    @pl.when(is_prompt)
    def _():
        op_ref[...] = x3

    @pl.when(jnp.logical_not(is_prompt))
    def _():
        os_ref[...] = x3


def _combine(ys, rgate, x1t, p_p, p_s, g2, b2, wpg, bpg, wp, g3, b3, tf, d, alpha):
    tp, tsmp = p_p.shape[0], p_s.shape[0]
    nch = d // LANES
    n_pt = tp // tf
    prow, srow = _split_maps(n_pt)
    row = lambda i: (i, 0)
    fixed = lambda i: (0, 0)
    full = lambda a: pl.BlockSpec(a.shape, fixed)
    return pl.pallas_call(
        functools.partial(_combine_body, tf=tf, nch=nch, alpha=alpha, n_pt=n_pt),
        grid=((tp + tsmp) // tf,),
        in_specs=[pl.BlockSpec((tf * TOP_K * nch, LANES), row), pl.BlockSpec((tf, LANES), row),
                  pl.BlockSpec((tf * nch, LANES), row), pl.BlockSpec((tf, p_p.shape[1]), prow),
                  pl.BlockSpec((tf, p_s.shape[1]), srow)]
                 + [full(a) for a in (g2, b2, wpg, bpg, wp, g3, b3)],
        out_specs=[pl.BlockSpec((tf, d), prow), pl.BlockSpec((tf, d), srow)],
        out_shape=[jax.ShapeDtypeStruct((tp, d), F32), jax.ShapeDtypeStruct((tsmp, d), F32)],
        compiler_params=_params(1),
        name="combine_ple",
    )(ys, rgate, x1t, p_p, p_s, g2, b2, wpg, bpg, wp, g3, b3)


def kernel(x_prompt, x_sample, p_prompt, p_sample, cache_k, cache_v, state_conv, page_table, w_in, b_in, conv_w, conv_b, conv_ln_g, conv_ln_b, w_conv_out, w_attn_out, w_out, ln1_g, ln1_b, router_w, router_b, exp_w_gu, exp_b_gu, exp_w_down, exp_b_down, ln2_g, ln2_b, ple_w, ple_gate_w, ple_gate_b, ln3_g, ln3_b):
    bp, sp, d = x_prompt.shape
    bs, dec, _ = x_sample.shape
    depth = w_in.shape[0]
    d_conv = conv_w.shape[2]
    d_attn = w_attn_out.shape[1]
    nh = d_attn // HEAD_DIM
    n_exp = router_w.shape[2]
    tp, tsmp = bp * sp, bs * dec
    alpha = (2 * depth) ** 0.25
    assert 2 * n_exp <= LANES and d % LANES == 0

    ts = _pick_tile(512, tp, tsmp)
    tf = _pick_tile(256, tp, tsmp)
    tm = 256
    row2 = lambda a: a.reshape(1, -1)

    x_p, x_s = x_prompt.reshape(tp, d), x_sample.reshape(tsmp, d)
    slopes = jnp.asarray([2.0 ** (-8.0 * (h + 1) / nh) for h in range(nh)], dtype=F32)
    cache_kt = jnp.transpose(cache_k, (0, 1, 3, 4, 2))
    cache_vt = jnp.transpose(cache_v, (0, 1, 3, 4, 2))
    b_gu4 = exp_b_gu[:, :, None, :]
    b_dn4 = exp_b_down[:, :, None, :]

    k_p, v_p, c_p, k_s, v_s, c_s = [], [], [], [], [], []
    n_qkv = 2 * d_conv + 3 * d_attn
    for l in range(depth):
        u_p, q_p, k_pl, v_pl, u_s, q_s, k_sl, v_sl = _inproj(
            x_p, x_s, w_in[l][:, :n_qkv].astype(BF16), row2(b_in[l][:n_qkv]), ts, d_conv, d_attn)

        conv_args = (conv_w[l], row2(conv_b[l]), row2(conv_ln_g[l]), row2(conv_ln_b[l]))
        c_p_pre = _conv_prompt(u_p, *conv_args, bp, sp)
        u_s3 = u_s.reshape(bs, dec, d_conv)
        c_s_t = _conv_sample(jnp.transpose(u_s3, (1, 0, 2)), jnp.transpose(state_conv[l], (1, 0, 2)), *conv_args)
        c_s_pre = jnp.transpose(c_s_t, (1, 0, 2)).reshape(tsmp, d_conv)

        a_p_pre = _attn_prompt(slopes, q_p, k_pl, v_pl, bp, sp)
        a_s_pre = _attn_sample(slopes, page_table, q_s, k_sl, v_sl, cache_kt, cache_vt, l, bs, dec)

        rw_hi, rw_lo = _split_bf16(router_w[l])
        rw = jnp.zeros((d, LANES), BF16).at[:, :n_exp].set(rw_hi).at[:, n_exp:2 * n_exp].set(rw_lo)
        rb = jnp.zeros((1, LANES), F32).at[0, :n_exp].set(router_b[l])
        x1t, ridx, rgate = _merge(
            x_p, x_s, c_p_pre, c_s_pre, a_p_pre, a_s_pre, w_in[l][:, n_qkv:].astype(BF16), row2(b_in[l][n_qkv:]),
            w_conv_out[l].astype(BF16), w_attn_out[l].astype(BF16), w_out[l].astype(BF16),
            row2(ln1_g[l]), row2(ln1_b[l]), rw, rb, ts, alpha, n_exp)

        ys = _moe(*_route(ridx, n_exp, tm), x1t, exp_w_gu, b_gu4, exp_w_down, b_dn4, l, tm, d)
        x_p, x_s = _combine(
            ys, rgate, x1t, p_prompt[l].reshape(tp, -1), p_sample[l].reshape(tsmp, -1), row2(ln2_g[l]), row2(ln2_b[l]),
            ple_gate_w[l].astype(BF16), row2(ple_gate_b[l]), ple_w[l].astype(BF16), row2(ln3_g[l]), row2(ln3_b[l]),
            tf, d, alpha)

        hist = conv_w.shape[1] - 1
        k_p.append(k_pl.reshape(bp, sp, nh, HEAD_DIM))
        v_p.append(v_pl.reshape(bp, sp, nh, HEAD_DIM))
        c_p.append(u_p.reshape(bp, sp, d_conv)[:, sp - hist:, :])
        k_s.append(k_sl.reshape(bs, dec, nh, HEAD_DIM))
        v_s.append(v_sl.reshape(bs, dec, nh, HEAD_DIM))
        c_s.append(jnp.concatenate([state_conv[l], u_s3], axis=1)[:, -hist:, :])

    return (x_p.reshape(bp, sp, d), x_s.reshape(bs, dec, d), jnp.stack(k_p), jnp.stack(v_p), jnp.stack(c_p),
            jnp.stack(k_s), jnp.stack(v_s), jnp.stack(c_s))
```

```python
import functools

import jax
import jax.numpy as jnp
from jax import lax
from jax.experimental import pallas as pl
from jax.experimental.pallas import tpu as pltpu

F32 = jnp.float32
BF16 = jnp.bfloat16

HEAD_DIM = 64
MOBA_BLOCK = 256
MOBA_TOPK = 3
PAGE_SIZE = 128
TOP_K = 4
SWIGLU_ALPHA = 1.702
SWIGLU_LIMIT = 7.0
LN_EPS = 1e-5

LANES = 128
SUBLANES = 8
VMEM_LIMIT_BYTES = 56 * 1024 * 1024
DMA_ISSUE_UNROLL = 8

NT_DIMS = (((1,), (1,)), ((), ()))


def _params(n_axes, vmem=None):
    return pltpu.CompilerParams(
        dimension_semantics=("arbitrary",) * n_axes,
        vmem_limit_bytes=VMEM_LIMIT_BYTES if vmem is None else vmem,
    )


def _pick_tile(cap, *sizes):
    t = cap
    while any(s % t for s in sizes):
        t //= 2
    return t


def _layer_norm(x, g, b):
    mu = jnp.mean(x, axis=-1, keepdims=True)
    xc = x - mu
    var = jnp.mean(xc * xc, axis=-1, keepdims=True)
    return xc * lax.rsqrt(var + LN_EPS) * g + b


def _split_bf16(a):
    hi = a.astype(BF16)
    lo = (a - hi.astype(F32)).astype(BF16)
    return hi, lo


def _dot3_nt(a, b):
    ah, al = _split_bf16(a)
    bh, bl = _split_bf16(b)
    f = lambda x, y: lax.dot_general(x, y, NT_DIMS, preferred_element_type=F32)
    return f(ah, bh) + (f(ah, bl) + f(al, bh))


def _bdot(a, b):
    return jnp.dot(a, b, preferred_element_type=F32)


def _split_maps(n_pt):
    return (lambda i: (jnp.minimum(i, n_pt - 1), 0)), (lambda i: (jnp.maximum(i - n_pt, 0), 0))


def _inproj_body(xp_ref, xs_ref, w_ref, b_ref, *out_refs, d_conv, d_attn, n_pt):
    is_prompt = pl.program_id(0) < n_pt
    x = jnp.where(is_prompt, xp_ref[...], xs_ref[...]).astype(BF16)

    def proj(lo, n):
        return _bdot(x, w_ref[:, lo:lo + n]) + b_ref[:, lo:lo + n]

    o = 2 * d_conv
    vals = (proj(0, d_conv) * jax.nn.sigmoid(proj(d_conv, d_conv)),
            proj(o, d_attn), proj(o + d_attn, d_attn), proj(o + 2 * d_attn, d_attn))

    @pl.when(is_prompt)
    def _():
        for ref, val in zip(out_refs[0:4], vals):
            ref[...] = val

    @pl.when(jnp.logical_not(is_prompt))
    def _():
        for ref, val in zip(out_refs[4:8], vals):
            ref[...] = val


def _inproj(x_p, x_s, w_bf, b, ts, d_conv, d_attn):
    tp, d = x_p.shape
    tsmp = x_s.shape[0]
    n = w_bf.shape[1]
    n_pt = tp // ts
    prow, srow = _split_maps(n_pt)
    fixed = lambda i: (0, 0)
    widths = (d_conv, d_attn, d_attn, d_attn)
    return pl.pallas_call(
        functools.partial(_inproj_body, d_conv=d_conv, d_attn=d_attn, n_pt=n_pt),
        grid=((tp + tsmp) // ts,),
        in_specs=[pl.BlockSpec((ts, d), prow), pl.BlockSpec((ts, d), srow), pl.BlockSpec((d, n), fixed),
                  pl.BlockSpec((1, n), fixed)],
        out_specs=[pl.BlockSpec((ts, w), prow) for w in widths] + [pl.BlockSpec((ts, w), srow) for w in widths],
        out_shape=[jax.ShapeDtypeStruct((tp, w), F32) for w in widths]
                  + [jax.ShapeDtypeStruct((tsmp, w), F32) for w in widths],
        compiler_params=_params(1),
        name="inproj",
    )(x_p, x_s, w_bf, b)


def _conv_prompt_body(u_ref, w_ref, cb_ref, g_ref, b_ref, o_ref, win_ref, sh_ref, *, ts, width, halo, rc):
    s = pl.program_id(1)
    c = u_ref.shape[1]
    span = halo + ts

    @pl.when(s == 0)
    def _():
        win_ref[0:halo, :] = jnp.zeros((halo, c), F32)

    @pl.when(s > 0)
    def _():
        win_ref[0:halo, :] = win_ref[ts:ts + halo, :]

    win_ref[halo:span, :] = u_ref[...]
    win_ref[span:span + SUBLANES, :] = jnp.zeros((SUBLANES, c), F32)
    for ph in range(1, SUBLANES):
        sh_ref[ph - 1] = win_ref[pl.ds(ph, span), :]
    off = halo - (width - 1)
    for r0 in range(0, ts, rc):
        acc = jnp.broadcast_to(cb_ref[...], (rc, c))
        for j in range(width):
            ph = (off + j) % SUBLANES
            base = r0 + off + j - ph
            rows = win_ref[base:base + rc, :] if ph == 0 else sh_ref[ph - 1, base:base + rc, :]
            acc = acc + rows * w_ref[j:j + 1, :]
        y = _layer_norm(acc, g_ref[...], b_ref[...])
        o_ref[r0:r0 + rc, :] = y * jax.nn.sigmoid(y)


def _conv_prompt(u, conv_w, conv_b, ln_g, ln_b, bp, sp):
    c = u.shape[1]
    width = conv_w.shape[0]
    halo = -(-(width - 1) // SUBLANES) * SUBLANES
    ts = _pick_tile(256, sp)
    assert ts >= halo
    ns = sp // ts
    tile = lambda b, s: (b * ns + s, 0)
    fixed = lambda b, s: (0, 0)
    return pl.pallas_call(
        functools.partial(_conv_prompt_body, ts=ts, width=width, halo=halo, rc=min(64, ts)),
        grid=(bp, ns),
        in_specs=[pl.BlockSpec((ts, c), tile), pl.BlockSpec((width, c), fixed)] + [pl.BlockSpec((1, c), fixed)] * 3,
        out_specs=pl.BlockSpec((ts, c), tile),
        out_shape=jax.ShapeDtypeStruct((bp * sp, c), F32),
        scratch_shapes=[pltpu.VMEM((halo + ts + SUBLANES, c), F32), pltpu.VMEM((SUBLANES - 1, halo + ts, c), F32)],
        compiler_params=_params(2),
        name="conv_prompt",
    )(u, conv_w, conv_b, ln_g, ln_b)


def _conv_sample_body(u_ref, h_ref, w_ref, cb_ref, g_ref, b_ref, o_ref, *, nb, dec, width):
    c = u_ref.shape[2]
    hist = width - 1

    def full_row(r):
        return h_ref[r] if r < hist else u_ref[r - hist]

    for t in range(dec):
        acc = jnp.broadcast_to(cb_ref[...], (nb, c))
        for j in range(width):
            acc = acc + full_row(t + j) * w_ref[j:j + 1, :]
        y = _layer_norm(acc, g_ref[...], b_ref[...])
        o_ref[t] = y * jax.nn.sigmoid(y)


def _conv_sample(u_t, hist_t, conv_w, conv_b, ln_g, ln_b):
    dec, bs, c = u_t.shape
    width = conv_w.shape[0]
    nb = min(32, bs)
    assert bs % nb == 0
    blockg = lambda g: (0, g, 0)
    fixed = lambda g: (0, 0)
    return pl.pallas_call(
        functools.partial(_conv_sample_body, nb=nb, dec=dec, width=width),
        grid=(bs // nb,),
        in_specs=[pl.BlockSpec((dec, nb, c), blockg), pl.BlockSpec((width - 1, nb, c), blockg),
                  pl.BlockSpec((width, c), fixed)] + [pl.BlockSpec((1, c), fixed)] * 3,
        out_specs=pl.BlockSpec((dec, nb, c), blockg),
        out_shape=jax.ShapeDtypeStruct((dec, bs, c), F32),
        compiler_params=_params(1),
        name="conv_sample",
    )(u_t, hist_t, conv_w, conv_b, ln_g, ln_b)


def _attn_prompt_body(slopes_ref, q_ref, k_ref, v_ref, o_ref, kmean_ref, bias_ref, *, blk, nb, n_sel, hd):
    hp = pl.program_id(1)
    qi = pl.program_id(2)

    nhl = LANES // hd
    rows = nhl * blk
    lane = lax.broadcasted_iota(jnp.int32, (1, LANES), 1)
    rowi = lax.broadcasted_iota(jnp.int32, (rows, 1), 0)
    rhead = rowi // blk
    rq = rowi % blk
    col = lax.broadcasted_iota(jnp.int32, (1, blk), 1)
    scale = HEAD_DIM ** -0.5
    neg_inf = -jnp.inf
    slope = jnp.zeros((rows, 1), F32)
    for hh in range(nhl):
        slope = jnp.where(rhead == hh, slopes_ref[hp * nhl + hh], slope)

    @pl.when(qi == 0)
    def _():
        kmean_ref[...] = jnp.zeros(kmean_ref.shape, F32)
        for j in range(nb):
            kmean_ref[j:j + 1, :] = jnp.mean(k_ref[j * blk:(j + 1) * blk, :], axis=0, keepdims=True)
        bias_ref[...] = slope * (rq - col).astype(F32)

    q = q_ref[...]
    q2 = jnp.where(lane // hd == rhead, jnp.concatenate([q] * nhl, axis=0), 0.0)
    q2b = (q2 * scale).astype(BF16)

    nbp = 2 * SUBLANES
    gate = _dot3_nt(kmean_ref[0:nbp, :], q2)
    jrow = lax.broadcasted_iota(jnp.int32, (nbp, 1), 0)
    rank = jnp.zeros((nbp, rows), F32)
    for jj in range(nb):
        gj = gate[jj:jj + 1, :]
        beats = (jj < qi) & ((gj > gate) | ((gj == gate) & (jj < jrow)))
        rank = rank + beats.astype(F32)
    sel_t = jnp.where((rank < n_sel) & (jrow < qi), 1.0, 0.0).astype(BF16)
    eye = jnp.where(jrow == lane, 1.0, 0.0).astype(BF16)
    sel = lax.dot_general(sel_t, eye, (((0,), (0,)), ((), ())), preferred_element_type=F32)

    def scores(j):
        kb = k_ref[pl.ds(pl.multiple_of(j * blk, blk), blk), :].astype(BF16)
        return lax.dot_general(q2b, kb, NT_DIMS, preferred_element_type=F32) - bias_ref[...]

    def pv(p, j):
        vb = v_ref[pl.ds(pl.multiple_of(j * blk, blk), blk), :].astype(BF16)
        return _bdot(p.astype(BF16), vb)

    s = jnp.where(rq >= col, scores(qi), neg_inf)
    m = jnp.max(s, axis=1, keepdims=True)
    p = jnp.exp(s - m)
    l = jnp.sum(p, axis=1, keepdims=True)
    acc = pv(p, qi)

    def body(j, carry):
        m, l, acc = carry
        pk = jnp.sum(jnp.where(lane == j, sel, 0.0), axis=1, keepdims=True) > 0.5
        off = slope * ((qi - j) * blk).astype(F32)
        s = scores(j)
        m_new = jnp.maximum(m, jnp.where(pk, jnp.max(s, axis=1, keepdims=True) - off, neg_inf))
        a = jnp.exp(m - m_new)
        p = jnp.exp(s - jnp.where(pk, m_new + off, jnp.inf))
        return m_new, a * l + jnp.sum(p, axis=1, keepdims=True), a * acc + pv(p, j)

    m, l, acc = lax.fori_loop(0, qi, body, (m, l, acc))
    res = acc / l
    o = res[0:blk, :]
    for hh in range(1, nhl):
        o = jnp.where(lane // hd == hh, res[hh * blk:(hh + 1) * blk, :], o)
    o_ref[...] = o


def _attn_prompt(slopes, q, k, v, bp, sp):
    da = q.shape[1]
    blk = MOBA_BLOCK
    nb = sp // blk
    assert sp % blk == 0 and nb <= 2 * SUBLANES and da % LANES == 0
    qtile = lambda b, hp, qi: (b * nb + qi, hp)
    seq = lambda b, hp, qi: (b, hp)
    return pl.pallas_call(
        functools.partial(_attn_prompt_body, blk=blk, nb=nb, n_sel=min(MOBA_TOPK, nb), hd=HEAD_DIM),
        grid=(bp, da // LANES, nb),
        in_specs=[pl.BlockSpec(memory_space=pltpu.SMEM), pl.BlockSpec((blk, LANES), qtile),
                  pl.BlockSpec((sp, LANES), seq), pl.BlockSpec((sp, LANES), seq)],
        out_specs=pl.BlockSpec((blk, LANES), qtile),
        out_shape=jax.ShapeDtypeStruct((bp * sp, da), F32),
        scratch_shapes=[pltpu.VMEM((LANES, LANES), F32), pltpu.VMEM((LANES // HEAD_DIM * blk, blk), F32)],
        compiler_params=_params(3),
        name="attn_prompt",
    )(slopes, q, k, v)


def _attn_sample_body(pt_ref, slopes_ref, q_ref, kn_ref, vn_ref, *refs, nh, hd, dec, nbk, n_pages, blk, n_sel, past_len):
    del pt_ref
    k_refs = refs[0:n_pages]
    v_refs = refs[n_pages:2 * n_pages]
    o_ref = refs[2 * n_pages]
    acc_ref = refs[2 * n_pages + 1]
    da = nh * hd
    nr = nh * dec
    ppb = blk // PAGE_SIZE
    scale = HEAD_DIM ** -0.5
    lane = lax.broadcasted_iota(jnp.int32, (1, LANES), 1)
    rowi = lax.broadcasted_iota(jnp.int32, (nr, 1), 0)
    slope = jnp.zeros((nr, 1), F32)
    for h in range(nh):
        slope = jnp.where(rowi // dec == h, slopes_ref[h], slope)
    tq = rowi % dec

    q = q_ref[...]
    hl = lax.broadcasted_iota(jnp.int32, (1, da), 1) // hd
    qbd = jnp.where(rowi // dec == hl, jnp.concatenate([q] * nh, axis=0), 0.0)
    qh, ql = _split_bf16(qbd)
    q2 = jnp.concatenate([qh, ql], axis=0)

    cols = [jnp.sum(qbd * kn_ref[t:t + 1, :], axis=1, keepdims=True) for t in range(dec)]
    s = jnp.concatenate(cols, axis=1) * scale
    tk = lax.broadcasted_iota(jnp.int32, (1, dec), 1)
    dist = (tq - tk).astype(F32)
    s = jnp.where(dist >= 0, s - slope * dist, -jnp.inf)
    mo = jnp.max(s, axis=1, keepdims=True)
    p = jnp.exp(s - mo)
    lo = jnp.sum(p, axis=1, keepdims=True)
    ao = jnp.zeros((nr, da), F32)
    for t in range(dec):
        ao = ao + p[:, t:t + 1] * vn_ref[t:t + 1, :]

    g = jnp.zeros((nr, LANES), F32)
    ms = jnp.zeros((nr, LANES), F32)
    ls = jnp.zeros((nr, LANES), F32)
    for j in range(nbk):
        kt = jnp.concatenate([k_refs[ppb * j + o][...].reshape(da, PAGE_SIZE) for o in range(ppb)], axis=1).astype(BF16)
        vt = jnp.concatenate([v_refs[ppb * j + o][...].reshape(da, PAGE_SIZE) for o in range(ppb)], axis=1).astype(BF16)
        s2 = _bdot(q2, kt)
        raw = s2[0:nr, :] + s2[nr:2 * nr, :]
        gate = jnp.mean(raw, axis=1, keepdims=True)
        kpos = j * blk + lax.broadcasted_iota(jnp.int32, (1, blk), 1)
        dist = (past_len + tq - kpos).astype(F32)
        s = raw * scale - slope * dist
        m = jnp.max(s, axis=1, keepdims=True)
        p = jnp.exp(s - m)
        acc_ref[j] = lax.dot_general(p.astype(BF16), vt, NT_DIMS, preferred_element_type=F32)
        g = jnp.where(lane == j, gate, g)
        ms = jnp.where(lane == j, m, ms)
        ls = jnp.where(lane == j, jnp.sum(p, axis=1, keepdims=True), ls)

    rank = jnp.zeros((nr, LANES), F32)
    for jj in range(nbk):
        gj = g[:, jj:jj + 1]
        rank = rank + ((gj > g) | ((gj == g) & (jj < lane))).astype(F32)
    sel = (rank < n_sel) & (lane < nbk)
    mx = jnp.maximum(mo, jnp.max(jnp.where(sel, ms, -jnp.inf), axis=1, keepdims=True))
    w = jnp.where(sel, jnp.exp(jnp.where(sel, ms, mx) - mx), 0.0)
    wo = jnp.exp(mo - mx)
    den = wo * lo + jnp.sum(w * ls, axis=1, keepdims=True)
    num = wo * ao
    for jj in range(nbk):
        num = num + w[:, jj:jj + 1] * acc_ref[jj]
    res = num / den
    out = jnp.zeros((dec, da), F32)
    for h in range(nh):
        out = jnp.where(hl == h, res[h * dec:(h + 1) * dec, :], out)
    o_ref[...] = out


def _attn_sample(slopes, page_table, q, k, v, cache_kt, cache_vt, layer, bs, dec):
    da = q.shape[1]
    nh = da // HEAD_DIM
    n_pages = page_table.shape[1]
    past_len = n_pages * PAGE_SIZE
    blk = MOBA_BLOCK
    assert blk % PAGE_SIZE == 0 and past_len % blk == 0
    nbk = past_len // blk
    nb_total = -(-(past_len + dec) // blk)
    assert nb_total == nbk + 1 and nbk <= LANES
    nr = nh * dec
    new_rows = lambda b, pt: (b, 0)
    page_spec = lambda pg: pl.BlockSpec((None, None, nh, HEAD_DIM, PAGE_SIZE),
                                        lambda b, pt: (layer, pt[b, pg], 0, 0, 0))
    grid_spec = pltpu.PrefetchScalarGridSpec(
        num_scalar_prefetch=1,
        grid=(bs,),
        in_specs=[pl.BlockSpec(memory_space=pltpu.SMEM)] + [pl.BlockSpec((dec, da), new_rows)] * 3
                 + [page_spec(pg) for pg in range(n_pages)] * 2,
        out_specs=pl.BlockSpec((dec, da), lambda b, pt: (b, 0)),
        scratch_shapes=[pltpu.VMEM((nbk, nr, da), F32)],
    )
    return pl.pallas_call(
        functools.partial(_attn_sample_body, nh=nh, hd=HEAD_DIM, dec=dec, nbk=nbk, n_pages=n_pages, blk=blk,
                          n_sel=min(MOBA_TOPK, nb_total), past_len=past_len),
        grid_spec=grid_spec,
        out_shape=jax.ShapeDtypeStruct((bs * dec, da), F32),
        compiler_params=_params(1),
        name="attn_sample",
    )(page_table, slopes, q, k, v, *([cache_kt] * n_pages), *([cache_vt] * n_pages))


def _merge_body(xp_ref, xs_ref, cp_ref, cs_ref, ap_ref, as_ref, wg_ref, bg_ref, wco_ref, wao_ref, wo_ref, g1_ref, b1_ref,
                rw_ref, rb_ref, x1_ref, ridx_ref, rgate_ref, *, alpha, n_exp, ts, d, n_pt):
    is_prompt = pl.program_id(0) < n_pt
    c_pre = jnp.where(is_prompt, cp_ref[...], cs_ref[...])
    a_pre = jnp.where(is_prompt, ap_ref[...], as_ref[...])
    x = jnp.where(is_prompt, xp_ref[...], xs_ref[...])
    xb = x.astype(BF16)
    gc = jax.nn.sigmoid(_bdot(xb, wg_ref[:, 0:d]) + bg_ref[:, 0:d])
    ga = jax.nn.sigmoid(_bdot(xb, wg_ref[:, d:2 * d]) + bg_ref[:, d:2 * d])
    c = _bdot(c_pre.astype(BF16), wco_ref[...])
    a = _bdot(a_pre.astype(BF16), wao_ref[...])
    merged = gc * c + ga * a
    m = _bdot(merged.astype(BF16), wo_ref[...])
    x1 = _layer_norm(alpha * x + m, g1_ref[...], b1_ref[...])
    nch = d // LANES
    for ch in range(nch):
        x1_ref[pl.ds(ch, ts, stride=nch), :] = x1[:, ch * LANES:(ch + 1) * LANES]

    xh, xl = _split_bf16(x1)
    tot = _bdot(xh, rw_ref[...]) + _bdot(xl, rw_ref[...])
    lane = lax.broadcasted_iota(jnp.int32, (1, LANES), 1)
    logits = tot + pltpu.roll(tot, LANES - n_exp, axis=1) + rb_ref[...]
    logits = jnp.where(lane < n_exp, logits, -jnp.inf)
    vals, idxs = [], []
    for _ in range(TOP_K):
        mx = jnp.max(logits, axis=1, keepdims=True)
        ix = jnp.min(jnp.where(logits == mx, lane, LANES), axis=1, keepdims=True)
        vals.append(mx)
        idxs.append(ix)
        logits = jnp.where(lane == ix, -jnp.inf, logits)
    es = [jnp.exp(v - vals[0]) for v in vals]
    den = es[0]
    for e in es[1:]:
        den = den + e
    ridx = jnp.zeros((ts, LANES), jnp.int32)
    rgate = jnp.zeros((ts, LANES), F32)
    for kk in range(TOP_K):
        ridx = jnp.where(lane == kk, idxs[kk], ridx)
        rgate = jnp.where(lane == kk, es[kk] / den, rgate)
    ridx_ref[...] = ridx
    rgate_ref[...] = rgate


def _merge(x_p, x_s, c_p, c_s, a_p, a_s, wg, bg, wco, wao, wo, g1, b1, rw, rb, ts, alpha, n_exp):
    d = x_p.shape[1]
    t = x_p.shape[0] + x_s.shape[0]
    dc = c_p.shape[1]
    da = a_p.shape[1]
    n_pt = x_p.shape[0] // ts
    prow, srow = _split_maps(n_pt)
    nch = d // LANES
    row = lambda i: (i, 0)
    fixed = lambda i: (0, 0)
    full = lambda a: pl.BlockSpec(a.shape, fixed)
    return pl.pallas_call(
        functools.partial(_merge_body, alpha=alpha, n_exp=n_exp, ts=ts, d=d, n_pt=n_pt),
        grid=(t // ts,),
        in_specs=[pl.BlockSpec((ts, d), prow), pl.BlockSpec((ts, d), srow),
                  pl.BlockSpec((ts, dc), prow), pl.BlockSpec((ts, dc), srow),
                  pl.BlockSpec((ts, da), prow), pl.BlockSpec((ts, da), srow)]
                 + [full(a) for a in (wg, bg, wco, wao, wo, g1, b1, rw, rb)],
        out_specs=[pl.BlockSpec((ts * nch, LANES), row), pl.BlockSpec((ts, LANES), row), pl.BlockSpec((ts, LANES), row)],
        out_shape=[jax.ShapeDtypeStruct((t * nch, LANES), F32), jax.ShapeDtypeStruct((t, LANES), jnp.int32),
                   jax.ShapeDtypeStruct((t, LANES), F32)],
        compiler_params=_params(1),
        name="merge_router",
    )(x_p, x_s, c_p, c_s, a_p, a_s, wg, bg, wco, wao, wo, g1, b1, rw, rb)


def _route(ridx, n_exp, tm):
    t = ridx.shape[0]
    p0 = t * TOP_K
    assert p0 % tm == 0 and TOP_K & (TOP_K - 1) == 0
    n_tiles = p0 // tm
    n_items = n_tiles + n_exp - 1
    shift = (p0 - 1).bit_length()
    assert n_exp << shift < 2 ** 31
    i32 = lambda a: a.astype(jnp.int32)
    keys = (ridx[:, :TOP_K].reshape(p0) << shift) | jnp.arange(p0, dtype=jnp.int32)
    keys = jnp.sort(keys)
    order = keys & ((1 << shift) - 1)
    bounds = jnp.arange(n_exp + 1, dtype=jnp.int32) << shift
    first = i32(jnp.sum(i32(keys[None, :] < bounds[:, None]), axis=1))
    start, end = first[:-1], first[1:]
    n_items_e = jnp.where(end > start, (end - 1) // tm - start // tm + 1, 0)
    item_end = jnp.cumsum(n_items_e)
    item_start = item_end - n_items_e
    total = item_end[-1]
    ids = jnp.arange(n_items, dtype=jnp.int32)
    idc = jnp.minimum(ids, total - 1)
    it_exp = i32(jnp.sum(i32(item_end[None, :] <= idc[:, None]), axis=1))
    onehot = i32(it_exp[:, None] == jnp.arange(n_exp, dtype=jnp.int32)[None, :])
    pick = lambda table: jnp.sum(onehot * table[None, :], axis=1)
    it_start, it_end = pick(start), pick(end)
    it_tile = it_start // tm + (idc - pick(item_start))
    lo = jnp.maximum(it_start, it_tile * tm) - it_tile * tm
    hi = jnp.minimum(it_end, (it_tile + 1) * tm) - it_tile * tm
    valid = ids < total
    return (i32(it_tile), it_exp, i32(jnp.where(valid, lo, 0)), i32(jnp.where(valid, hi, 0)),
            i32(total).reshape(1), i32(order))


def _moe_body(tile_ref, exp_ref, lo_ref, hi_ref, total_ref, order_ref, x_hbm, wgu_ref, bgu_ref, wdn_ref, bdn_ref, ys_hbm,
              xbuf0, xbuf1, ybuf0, ybuf1, gsem, ssem, wgu_bf, wdn_bf, *, tm, d, de, nch, n_tiles, n_items):
    i = pl.program_id(0)
    total = total_ref[0]
    valid = i < total
    tile = tile_ref[i]
    prev = jnp.maximum(i - 1, 0)
    nxt = jnp.minimum(i + 1, n_items - 1)
    first = (i == 0) | (tile != tile_ref[prev])
    last = (i == total - 1) | (tile != tile_ref[nxt])
    new_expert = (i == 0) | (exp_ref[i] != exp_ref[prev])
    xbufs = (xbuf0, xbuf1)
    ybufs = (ybuf0, ybuf1)

    def gather_copy(pair, r, slot):
        tok = lax.shift_right_logical(pair, TOP_K.bit_length() - 1)
        return pltpu.make_async_copy(
            x_hbm.at[pl.ds(pl.multiple_of(tok * nch, nch), nch), :],
            xbufs[slot].at[pl.ds(pl.multiple_of(r * nch, nch), nch), :],
            gsem.at[slot])

    def scatter_copy(pair, r, slot):
        return pltpu.make_async_copy(
            ybufs[slot].at[pl.ds(pl.multiple_of(r * nch, nch), nch), :],
            ys_hbm.at[pl.ds(pl.multiple_of(pair * nch, nch), nch), :],
            ssem.at[slot])

    def issue(copy, t, slot):
        def body(g, carry):
            for u in range(DMA_ISSUE_UNROLL):
                r = g * DMA_ISSUE_UNROLL + u
                copy(order_ref[t * tm + r], r, slot).start()
            return carry
        lax.fori_loop(0, tm // DMA_ISSUE_UNROLL, body, 0)

    def wait_all(copy, slot):
        if copy is gather_copy:
            whole = pltpu.make_async_copy(x_hbm.at[pl.ds(0, tm * nch), :], xbufs[slot], gsem.at[slot])
        else:
            whole = pltpu.make_async_copy(ybufs[slot], ys_hbm.at[pl.ds(0, tm * nch), :], ssem.at[slot])
        whole.wait()

    @pl.when(i == 0)
    def _():
        issue(gather_copy, 0, 0)

    @pl.when(valid & new_expert)
    def _():
        rchunk = 64
        def body(c, carry):
            r0 = pl.multiple_of(c * rchunk, rchunk)
            wgu_bf[pl.ds(r0, rchunk), :] = wgu_ref[pl.ds(r0, rchunk), :].astype(BF16)
            return carry
        lax.fori_loop(0, d // rchunk, body, 0)
        def body2(c, carry):
            r0 = pl.multiple_of(c * rchunk, rchunk)
            wdn_bf[pl.ds(r0, rchunk), :] = wdn_ref[pl.ds(r0, rchunk), :].astype(BF16)
            return carry
        lax.fori_loop(0, de // rchunk, body2, 0)

    for slot in (0, 1):
        @pl.when(valid & (tile % 2 == slot))
        def _(slot=slot):
            @pl.when(first)
            def _():
                @pl.when(tile + 1 < n_tiles)
                def _():
                    issue(gather_copy, tile + 1, 1 - slot)
                wait_all(gather_copy, slot)
                @pl.when(tile >= 2)
                def _():
                    wait_all(scatter_copy, slot)

            xbuf, ybuf = xbufs[slot], ybufs[slot]
            x = jnp.concatenate([xbuf[pl.ds(ch, tm, stride=nch), :] for ch in range(nch)], axis=1).astype(BF16)
            h = _bdot(x, wgu_bf[...]) + bgu_ref[...]
            gate = jnp.minimum(h[:, 0:de], SWIGLU_LIMIT)
            up = jnp.clip(h[:, de:2 * de], -SWIGLU_LIMIT, SWIGLU_LIMIT)
            act = (up + 1.0) * (gate * jax.nn.sigmoid(SWIGLU_ALPHA * gate))
            y = _bdot(act.astype(BF16), wdn_bf[...]) + bdn_ref[...]
            rows = lax.broadcasted_iota(jnp.int32, (tm, 1), 0)
            y = jnp.where((rows >= lo_ref[i]) & (rows < hi_ref[i]), y, 0.0)

            @pl.when(first)
            def _():
                for ch in range(nch):
                    ybuf[pl.ds(ch, tm, stride=nch), :] = y[:, ch * LANES:(ch + 1) * LANES]

            @pl.when(jnp.logical_not(first))
            def _():
                for ch in range(nch):
                    ybuf[pl.ds(ch, tm, stride=nch), :] = ybuf[pl.ds(ch, tm, stride=nch), :] + y[:, ch * LANES:(ch + 1) * LANES]

            @pl.when(last)
            def _():
                issue(scatter_copy, tile, slot)

    @pl.when(i == total - 1)
    def _():
        for tl in range(max(n_tiles - 2, 0), n_tiles):
            wait_all(scatter_copy, tl % 2)


def _moe(it_tile, it_exp, it_lo, it_hi, total, order, x1t, w_gu, b_gu, w_dn, b_dn, layer, tm, d):
    n_items = it_tile.shape[0]
    p0 = order.shape[0]
    n_tiles = p0 // tm
    de = w_dn.shape[2]
    nch = d // LANES
    wmap = lambda i, tl, ex, lo, hi, tot, od: (layer, ex[i], 0, 0)
    grid_spec = pltpu.PrefetchScalarGridSpec(
        num_scalar_prefetch=6,
        grid=(n_items,),
        in_specs=[pl.BlockSpec(memory_space=pl.ANY),
                  pl.BlockSpec((None, None, d, 2 * de), wmap), pl.BlockSpec((None, None, 1, 2 * de), wmap),
                  pl.BlockSpec((None, None, de, d), wmap), pl.BlockSpec((None, None, 1, d), wmap)],
        out_specs=pl.BlockSpec(memory_space=pl.ANY),
        scratch_shapes=[pltpu.VMEM((tm * nch, LANES), F32)] * 4 + [pltpu.SemaphoreType.DMA((2,))] * 2
                       + [pltpu.VMEM((d, 2 * de), BF16), pltpu.VMEM((de, d), BF16)],
    )
    return pl.pallas_call(
        functools.partial(_moe_body, tm=tm, d=d, de=de, nch=nch, n_tiles=n_tiles, n_items=n_items),
        grid_spec=grid_spec,
        out_shape=jax.ShapeDtypeStruct((p0 * nch, LANES), F32),
        compiler_params=_params(1),
        name="moe_experts",
    )(it_tile, it_exp, it_lo, it_hi, total, order, x1t, w_gu, b_gu, w_dn, b_dn)


def _combine_body(ys_ref, rg_ref, x1_ref, pp_ref, ps_ref, g2_ref, b2_ref, wpg_ref, bpg_ref, wp_ref, g3_ref, b3_ref,
                  op_ref, os_ref, *, tf, nch, alpha, n_pt):
    is_prompt = pl.program_id(0) < n_pt
    ple = jnp.where(is_prompt, pp_ref[...], ps_ref[...])
    rg = rg_ref[...]
    gk = [jnp.broadcast_to(rg[:, kk:kk + 1], (tf, LANES)) for kk in range(TOP_K)]
    stride = TOP_K * nch
    fs, xs = [], []
    for ch in range(nch):
        f = gk[0] * ys_ref[pl.ds(ch, tf, stride=stride), :]
        for kk in range(1, TOP_K):
            f = f + gk[kk] * ys_ref[pl.ds(kk * nch + ch, tf, stride=stride), :]
        fs.append(f)
        xs.append(x1_ref[pl.ds(ch, tf, stride=nch), :])
    f = jnp.concatenate(fs, axis=1)
    x1 = jnp.concatenate(xs, axis=1)
    x2 = _layer_norm(alpha * x1 + f, g2_ref[...], b2_ref[...])
    e = jax.nn.sigmoid(_bdot(x2.astype(BF16), wpg_ref[...]) + bpg_ref[...]) * _bdot(ple.astype(BF16), wp_ref[...])
    x3 = _layer_norm(alpha * x2 + e, g3_ref[...], b3_ref[...])

    @pl.when(is_prompt)
    def _():
        op_ref[...] = x3

    @pl.when(jnp.logical_not(is_prompt))
    def _():
        os_ref[...] = x3


def _combine(ys, rgate, x1t, p_p, p_s, g2, b2, wpg, bpg, wp, g3, b3, tf, d, alpha):
    tp, tsmp = p_p.shape[0], p_s.shape[0]
    nch = d // LANES
    n_pt = tp // tf
    prow, srow = _split_maps(n_pt)
    row = lambda i: (i, 0)
    fixed = lambda i: (0, 0)
    full = lambda a: pl.BlockSpec(a.shape, fixed)
    return pl.pallas_call(
        functools.partial(_combine_body, tf=tf, nch=nch, alpha=alpha, n_pt=n_pt),
        grid=((tp + tsmp) // tf,),
        in_specs=[pl.BlockSpec((tf * TOP_K * nch, LANES), row), pl.BlockSpec((tf, LANES), row),
                  pl.BlockSpec((tf * nch, LANES), row), pl.BlockSpec((tf, p_p.shape[1]), prow),
                  pl.BlockSpec((tf, p_s.shape[1]), srow)]
                 + [full(a) for a in (g2, b2, wpg, bpg, wp, g3, b3)],
        out_specs=[pl.BlockSpec((tf, d), prow), pl.BlockSpec((tf, d), srow)],
        out_shape=[jax.ShapeDtypeStruct((tp, d), F32), jax.ShapeDtypeStruct((tsmp, d), F32)],
        compiler_params=_params(1),
        name="combine_ple",
    )(ys, rgate, x1t, p_p, p_s, g2, b2, wpg, bpg, wp, g3, b3)


def kernel(x_prompt, x_sample, p_prompt, p_sample, cache_k, cache_v, state_conv, page_table, w_in, b_in, conv_w, conv_b, conv_ln_g, conv_ln_b, w_conv_out, w_attn_out, w_out, ln1_g, ln1_b, router_w, router_b, exp_w_gu, exp_b_gu, exp_w_down, exp_b_down, ln2_g, ln2_b, ple_w, ple_gate_w, ple_gate_b, ln3_g, ln3_b):
    bp, sp, d = x_prompt.shape
    bs, dec, _ = x_sample.shape
    depth = w_in.shape[0]
    d_conv = conv_w.shape[2]
    d_attn = w_attn_out.shape[1]
    nh = d_attn // HEAD_DIM
    n_exp = router_w.shape[2]
    tp, tsmp = bp * sp, bs * dec
    alpha = (2 * depth) ** 0.25
    assert 2 * n_exp <= LANES and d % LANES == 0

    ts = _pick_tile(512, tp, tsmp)
    tf = _pick_tile(256, tp, tsmp)
    tm = 256
    row2 = lambda a: a.reshape(1, -1)

    x_p, x_s = x_prompt.reshape(tp, d), x_sample.reshape(tsmp, d)
    slopes = jnp.asarray([2.0 ** (-8.0 * (h + 1) / nh) for h in range(nh)], dtype=F32)
    cache_kt = jnp.transpose(cache_k, (0, 1, 3, 4, 2))
    cache_vt = jnp.transpose(cache_v, (0, 1, 3, 4, 2))
    b_gu4 = exp_b_gu[:, :, None, :]
    b_dn4 = exp_b_down[:, :, None, :]

    k_p, v_p, c_p, k_s, v_s, c_s = [], [], [], [], [], []
    n_qkv = 2 * d_conv + 3 * d_attn
    for l in range(depth):
        u_p, q_p, k_pl, v_pl, u_s, q_s, k_sl, v_sl = _inproj(
            x_p, x_s, w_in[l][:, :n_qkv].astype(BF16), row2(b_in[l][:n_qkv]), ts, d_conv, d_attn)

        conv_args = (conv_w[l], row2(conv_b[l]), row2(conv_ln_g[l]), row2(conv_ln_b[l]))
        c_p_pre = _conv_prompt(u_p, *conv_args, bp, sp)
        u_s3 = u_s.reshape(bs, dec, d_conv)
        c_s_t = _conv_sample(jnp.transpose(u_s3, (1, 0, 2)), jnp.transpose(state_conv[l], (1, 0, 2)), *conv_args)
        c_s_pre = jnp.transpose(c_s_t, (1, 0, 2)).reshape(tsmp, d_conv)

        a_p_pre = _attn_prompt(slopes, q_p, k_pl, v_pl, bp, sp)
        a_s_pre = _attn_sample(slopes, page_table, q_s, k_sl, v_sl, cache_kt, cache_vt, l, bs, dec)

        rw_hi, rw_lo = _split_bf16(router_w[l])
        rw = jnp.zeros((d, LANES), BF16).at[:, :n_exp].set(rw_hi).at[:, n_exp:2 * n_exp].set(rw_lo)
        rb = jnp.zeros((1, LANES), F32).at[0, :n_exp].set(router_b[l])
        x1t, ridx, rgate = _merge(
            x_p, x_s, c_p_pre, c_s_pre, a_p_pre, a_s_pre, w_in[l][:, n_qkv:].astype(BF16), row2(b_in[l][n_qkv:]),
            w_conv_out[l].astype(BF16), w_attn_out[l].astype(BF16), w_out[l].astype(BF16),
            row2(ln1_g[l]), row2(ln1_b[l]), rw, rb, ts, alpha, n_exp)

        ys = _moe(*_route(ridx, n_exp, tm), x1t, exp_w_gu, b_gu4, exp_w_down, b_dn4, l, tm, d)
        x_p, x_s = _combine(
            ys, rgate, x1t, p_prompt[l].reshape(tp, -1), p_sample[l].reshape(tsmp, -1), row2(ln2_g[l]), row2(ln2_b[l]),
            ple_gate_w[l].astype(BF16), row2(ple_gate_b[l]), ple_w[l].astype(BF16), row2(ln3_g[l]), row2(ln3_b[l]),
            tf, d, alpha)

        hist = conv_w.shape[1] - 1
        k_p.append(k_pl.reshape(bp, sp, nh, HEAD_DIM))
        v_p.append(v_pl.reshape(bp, sp, nh, HEAD_DIM))
        c_p.append(u_p.reshape(bp, sp, d_conv)[:, sp - hist:, :])
        k_s.append(k_sl.reshape(bs, dec, nh, HEAD_DIM))
        v_s.append(v_sl.reshape(bs, dec, nh, HEAD_DIM))
        c_s.append(jnp.concatenate([state_conv[l], u_s3], axis=1)[:, -hist:, :])

    return (x_p.reshape(bp, sp, d), x_s.reshape(bs, dec, d), jnp.stack(k_p), jnp.stack(v_p), jnp.stack(c_p),
            jnp.stack(k_s), jnp.stack(v_s), jnp.stack(c_s))
```

```python
import functools

import jax
import jax.numpy as jnp
from jax import lax
from jax.experimental import pallas as pl
from jax.experimental.pallas import tpu as pltpu

F32 = jnp.float32
BF16 = jnp.bfloat16

HEAD_DIM = 64
MOBA_BLOCK = 256
MOBA_TOPK = 3
PAGE_SIZE = 128
TOP_K = 4
SWIGLU_ALPHA = 1.702
SWIGLU_LIMIT = 7.0
LN_EPS = 1e-5

LANES = 128
SUBLANES = 8
VMEM_LIMIT_BYTES = 56 * 1024 * 1024
DMA_ISSUE_UNROLL = 8

NT_DIMS = (((1,), (1,)), ((), ()))


def _params(n_axes, vmem=None):
    return pltpu.CompilerParams(
        dimension_semantics=("arbitrary",) * n_axes,
        vmem_limit_bytes=VMEM_LIMIT_BYTES if vmem is None else vmem,
    )


def _pick_tile(cap, *sizes):
    t = cap
    while any(s % t for s in sizes):
        t //= 2
    return t


def _layer_norm(x, g, b):
    mu = jnp.mean(x, axis=-1, keepdims=True)
    xc = x - mu
    var = jnp.mean(xc * xc, axis=-1, keepdims=True)
    return xc * lax.rsqrt(var + LN_EPS) * g + b


def _split_bf16(a):
    hi = a.astype(BF16)
    lo = (a - hi.astype(F32)).astype(BF16)
    return hi, lo


def _dot3_nt(a, b):
    ah, al = _split_bf16(a)
    bh, bl = _split_bf16(b)
    f = lambda x, y: lax.dot_general(x, y, NT_DIMS, preferred_element_type=F32)
    return f(ah, bh) + (f(ah, bl) + f(al, bh))


def _bdot(a, b):
    return jnp.dot(a, b, preferred_element_type=F32)


def _split_maps(n_pt):
    return (lambda i: (jnp.minimum(i, n_pt - 1), 0)), (lambda i: (jnp.maximum(i - n_pt, 0), 0))


def _inproj_body(xp_ref, xs_ref, w_ref, b_ref, *out_refs, d_conv, d_attn, n_pt):
    is_prompt = pl.program_id(0) < n_pt
    x = jnp.where(is_prompt, xp_ref[...], xs_ref[...]).astype(BF16)

    def proj(lo, n):
        return _bdot(x, w_ref[:, lo:lo + n]) + b_ref[:, lo:lo + n]

    o = 2 * d_conv
    vals = (proj(0, d_conv) * jax.nn.sigmoid(proj(d_conv, d_conv)),
            proj(o, d_attn), proj(o + d_attn, d_attn), proj(o + 2 * d_attn, d_attn))

    @pl.when(is_prompt)
    def _():
        for ref, val in zip(out_refs[0:4], vals):
            ref[...] = val

    @pl.when(jnp.logical_not(is_prompt))
    def _():
        for ref, val in zip(out_refs[4:8], vals):
            ref[...] = val


def _inproj(x_p, x_s, w_bf, b, ts, d_conv, d_attn):
    tp, d = x_p.shape
    tsmp = x_s.shape[0]
    n = w_bf.shape[1]
    n_pt = tp // ts
    prow, srow = _split_maps(n_pt)
    fixed = lambda i: (0, 0)
    widths = (d_conv, d_attn, d_attn, d_attn)
    return pl.pallas_call(
        functools.partial(_inproj_body, d_conv=d_conv, d_attn=d_attn, n_pt=n_pt),
        grid=((tp + tsmp) // ts,),
        in_specs=[pl.BlockSpec((ts, d), prow), pl.BlockSpec((ts, d), srow), pl.BlockSpec((d, n), fixed),
                  pl.BlockSpec((1, n), fixed)],
        out_specs=[pl.BlockSpec((ts, w), prow) for w in widths] + [pl.BlockSpec((ts, w), srow) for w in widths],
        out_shape=[jax.ShapeDtypeStruct((tp, w), F32) for w in widths]
                  + [jax.ShapeDtypeStruct((tsmp, w), F32) for w in widths],
        compiler_params=_params(1),
        name="inproj",
    )(x_p, x_s, w_bf, b)


def _conv_prompt_body(u_ref, w_ref, cb_ref, g_ref, b_ref, o_ref, win_ref, sh_ref, *, ts, width, halo, rc):
    s = pl.program_id(1)
    c = u_ref.shape[1]
    span = halo + ts

    @pl.when(s == 0)
    def _():
        win_ref[0:halo, :] = jnp.zeros((halo, c), F32)

    @pl.when(s > 0)
    def _():
        win_ref[0:halo, :] = win_ref[ts:ts + halo, :]

    win_ref[halo:span, :] = u_ref[...]
    win_ref[span:span + SUBLANES, :] = jnp.zeros((SUBLANES, c), F32)
    for ph in range(1, SUBLANES):
        sh_ref[ph - 1] = win_ref[pl.ds(ph, span), :]
    off = halo - (width - 1)
    for r0 in range(0, ts, rc):
        acc = jnp.broadcast_to(cb_ref[...], (rc, c))
        for j in range(width):
            ph = (off + j) % SUBLANES
            base = r0 + off + j - ph
            rows = win_ref[base:base + rc, :] if ph == 0 else sh_ref[ph - 1, base:base + rc, :]
            acc = acc + rows * w_ref[j:j + 1, :]
        y = _layer_norm(acc, g_ref[...], b_ref[...])
        o_ref[r0:r0 + rc, :] = y * jax.nn.sigmoid(y)


def _conv_prompt(u, conv_w, conv_b, ln_g, ln_b, bp, sp):
    c = u.shape[1]
    width = conv_w.shape[0]
    halo = -(-(width - 1) // SUBLANES) * SUBLANES
    ts = _pick_tile(256, sp)
    assert ts >= halo
    ns = sp // ts
    tile = lambda b, s: (b * ns + s, 0)
    fixed = lambda b, s: (0, 0)
    return pl.pallas_call(
        functools.partial(_conv_prompt_body, ts=ts, width=width, halo=halo, rc=min(64, ts)),
        grid=(bp, ns),
        in_specs=[pl.BlockSpec((ts, c), tile), pl.BlockSpec((width, c), fixed)] + [pl.BlockSpec((1, c), fixed)] * 3,
        out_specs=pl.BlockSpec((ts, c), tile),
        out_shape=jax.ShapeDtypeStruct((bp * sp, c), F32),
        scratch_shapes=[pltpu.VMEM((halo + ts + SUBLANES, c), F32), pltpu.VMEM((SUBLANES - 1, halo + ts, c), F32)],
        compiler_params=_params(2),
        name="conv_prompt",
    )(u, conv_w, conv_b, ln_g, ln_b)


def _conv_sample_body(u_ref, h_ref, w_ref, cb_ref, g_ref, b_ref, o_ref, *, nb, dec, width):
    c = u_ref.shape[2]
    hist = width - 1

    def full_row(r):
        return h_ref[r] if r < hist else u_ref[r - hist]

    for t in range(dec):
        acc = jnp.broadcast_to(cb_ref[...], (nb, c))
        for j in range(width):
            acc = acc + full_row(t + j) * w_ref[j:j + 1, :]
        y = _layer_norm(acc, g_ref[...], b_ref[...])
        o_ref[t] = y * jax.nn.sigmoid(y)


def _conv_sample(u_t, hist_t, conv_w, conv_b, ln_g, ln_b):
    dec, bs, c = u_t.shape
    width = conv_w.shape[0]
    nb = min(32, bs)
    assert bs % nb == 0
    blockg = lambda g: (0, g, 0)
    fixed = lambda g: (0, 0)
    return pl.pallas_call(
        functools.partial(_conv_sample_body, nb=nb, dec=dec, width=width),
        grid=(bs // nb,),
        in_specs=[pl.BlockSpec((dec, nb, c), blockg), pl.BlockSpec((width - 1, nb, c), blockg),
                  pl.BlockSpec((width, c), fixed)] + [pl.BlockSpec((1, c), fixed)] * 3,
        out_specs=pl.BlockSpec((dec, nb, c), blockg),
        out_shape=jax.ShapeDtypeStruct((dec, bs, c), F32),
        compiler_params=_params(1),
        name="conv_sample",
    )(u_t, hist_t, conv_w, conv_b, ln_g, ln_b)


def _attn_prompt_body(slopes_ref, q_ref, k_ref, v_ref, o_ref, kmean_ref, bias_ref, *, blk, nb, n_sel, hd):
    hp = pl.program_id(1)
    qi = pl.program_id(2)

    nhl = LANES // hd
    rows = nhl * blk
    lane = lax.broadcasted_iota(jnp.int32, (1, LANES), 1)
    rowi = lax.broadcasted_iota(jnp.int32, (rows, 1), 0)
    rhead = rowi // blk
    rq = rowi % blk
    col = lax.broadcasted_iota(jnp.int32, (1, blk), 1)
    scale = HEAD_DIM ** -0.5
    neg_inf = -jnp.inf
    slope = jnp.zeros((rows, 1), F32)
    for hh in range(nhl):
        slope = jnp.where(rhead == hh, slopes_ref[hp * nhl + hh], slope)

    @pl.when(qi == 0)
    def _():
        kmean_ref[...] = jnp.zeros(kmean_ref.shape, F32)
        for j in range(nb):
            kmean_ref[j:j + 1, :] = jnp.mean(k_ref[j * blk:(j + 1) * blk, :], axis=0, keepdims=True)
        bias_ref[...] = slope * (rq - col).astype(F32)

    q = q_ref[...]
    q2 = jnp.where(lane // hd == rhead, jnp.concatenate([q] * nhl, axis=0), 0.0)
    q2b = (q2 * scale).astype(BF16)

    nbp = 2 * SUBLANES
    gate = _dot3_nt(kmean_ref[0:nbp, :], q2)
    jrow = lax.broadcasted_iota(jnp.int32, (nbp, 1), 0)
    rank = jnp.zeros((nbp, rows), F32)
    for jj in range(nb):
        gj = gate[jj:jj + 1, :]
        beats = (jj < qi) & ((gj > gate) | ((gj == gate) & (jj < jrow)))
        rank = rank + beats.astype(F32)
    sel_t = jnp.where((rank < n_sel) & (jrow < qi), 1.0, 0.0).astype(BF16)
    eye = jnp.where(jrow == lane, 1.0, 0.0).astype(BF16)
    sel = lax.dot_general(sel_t, eye, (((0,), (0,)), ((), ())), preferred_element_type=F32)

    def scores(j):
        kb = k_ref[pl.ds(pl.multiple_of(j * blk, blk), blk), :].astype(BF16)
        return lax.dot_general(q2b, kb, NT_DIMS, preferred_element_type=F32) - bias_ref[...]

    def pv(p, j):
        vb = v_ref[pl.ds(pl.multiple_of(j * blk, blk), blk), :].astype(BF16)
        return _bdot(p.astype(BF16), vb)

    s = jnp.where(rq >= col, scores(qi), neg_inf)
    m = jnp.max(s, axis=1, keepdims=True)
    p = jnp.exp(s - m)
    l = jnp.sum(p, axis=1, keepdims=True)
    acc = pv(p, qi)

    def body(j, carry):
        m, l, acc = carry
        pk = jnp.sum(jnp.where(lane == j, sel, 0.0), axis=1, keepdims=True) > 0.5
        off = slope * ((qi - j) * blk).astype(F32)
        s = scores(j)
        m_new = jnp.maximum(m, jnp.where(pk, jnp.max(s, axis=1, keepdims=True) - off, neg_inf))
        a = jnp.exp(m - m_new)
        p = jnp.exp(s - jnp.where(pk, m_new + off, jnp.inf))
        return m_new, a * l + jnp.sum(p, axis=1, keepdims=True), a * acc + pv(p, j)

    m, l, acc = lax.fori_loop(0, qi, body, (m, l, acc))
    res = acc / l
    o = res[0:blk, :]
    for hh in range(1, nhl):
        o = jnp.where(lane // hd == hh, res[hh * blk:(hh + 1) * blk, :], o)
    o_ref[...] = o


def _attn_prompt(slopes, q, k, v, bp, sp):
    da = q.shape[1]
    blk = MOBA_BLOCK
    nb = sp // blk
    assert sp % blk == 0 and nb <= 2 * SUBLANES and da % LANES == 0
    qtile = lambda b, hp, qi: (b * nb + qi, hp)
    seq = lambda b, hp, qi: (b, hp)
    return pl.pallas_call(
        functools.partial(_attn_prompt_body, blk=blk, nb=nb, n_sel=min(MOBA_TOPK, nb), hd=HEAD_DIM),
        grid=(bp, da // LANES, nb),
        in_specs=[pl.BlockSpec(memory_space=pltpu.SMEM), pl.BlockSpec((blk, LANES), qtile),
                  pl.BlockSpec((sp, LANES), seq), pl.BlockSpec((sp, LANES), seq)],
        out_specs=pl.BlockSpec((blk, LANES), qtile),
        out_shape=jax.ShapeDtypeStruct((bp * sp, da), F32),
        scratch_shapes=[pltpu.VMEM((LANES, LANES), F32), pltpu.VMEM((LANES // HEAD_DIM * blk, blk), F32)],
        compiler_params=_params(3),
        name="attn_prompt",
    )(slopes, q, k, v)


def _attn_sample_body(pt_ref, slopes_ref, q_ref, kn_ref, vn_ref, *refs, nh, hd, dec, nbk, n_pages, blk, n_sel, past_len):
    del pt_ref
    k_refs = refs[0:n_pages]
    v_refs = refs[n_pages:2 * n_pages]
    o_ref = refs[2 * n_pages]
    acc_ref = refs[2 * n_pages + 1]
    da = nh * hd
    nr = nh * dec
    ppb = blk // PAGE_SIZE
    scale = HEAD_DIM ** -0.5
    lane = lax.broadcasted_iota(jnp.int32, (1, LANES), 1)
    rowi = lax.broadcasted_iota(jnp.int32, (nr, 1), 0)
    slope = jnp.zeros((nr, 1), F32)
    for h in range(nh):
        slope = jnp.where(rowi // dec == h, slopes_ref[h], slope)
    tq = rowi % dec

    q = q_ref[...]
    hl = lax.broadcasted_iota(jnp.int32, (1, da), 1) // hd
    qbd = jnp.where(rowi // dec == hl, jnp.concatenate([q] * nh, axis=0), 0.0)
    qh, ql = _split_bf16(qbd)
    q2 = jnp.concatenate([qh, ql], axis=0)

    cols = [jnp.sum(qbd * kn_ref[t:t + 1, :], axis=1, keepdims=True) for t in range(dec)]
    s = jnp.concatenate(cols, axis=1) * scale
    tk = lax.broadcasted_iota(jnp.int32, (1, dec), 1)
    dist = (tq - tk).astype(F32)
    s = jnp.where(dist >= 0, s - slope * dist, -jnp.inf)
    mo = jnp.max(s, axis=1, keepdims=True)
    p = jnp.exp(s - mo)
    lo = jnp.sum(p, axis=1, keepdims=True)
    ao = jnp.zeros((nr, da), F32)
    for t in range(dec):
        ao = ao + p[:, t:t + 1] * vn_ref[t:t + 1, :]

    g = jnp.zeros((nr, LANES), F32)
    ms = jnp.zeros((nr, LANES), F32)
    ls = jnp.zeros((nr, LANES), F32)
    def page_block(refs, j):
        return jnp.concatenate([refs[ppb * j + o][...].reshape(da, PAGE_SIZE) for o in range(ppb)], axis=1).astype(BF16)

    raws = []
    for j in range(nbk):
        s2 = _bdot(q2, page_block(k_refs, j))
        raws.append(s2[0:nr, :] + s2[nr:2 * nr, :])
    ps = []
    for j in range(nbk):
        gate = jnp.mean(raws[j], axis=1, keepdims=True)
        kpos = j * blk + lax.broadcasted_iota(jnp.int32, (1, blk), 1)
        dist = (past_len + tq - kpos).astype(F32)
        s = raws[j] * scale - slope * dist
        m = jnp.max(s, axis=1, keepdims=True)
        p = jnp.exp(s - m)
        ps.append(p.astype(BF16))
        g = jnp.where(lane == j, gate, g)
        ms = jnp.where(lane == j, m, ms)
        ls = jnp.where(lane == j, jnp.sum(p, axis=1, keepdims=True), ls)
    for j in range(nbk):
        acc_ref[j] = lax.dot_general(ps[j], page_block(v_refs, j), NT_DIMS, preferred_element_type=F32)

    rank = jnp.zeros((nr, LANES), F32)
    for jj in range(nbk):
        gj = g[:, jj:jj + 1]
        rank = rank + ((gj > g) | ((gj == g) & (jj < lane))).astype(F32)
    sel = (rank < n_sel) & (lane < nbk)
    mx = jnp.maximum(mo, jnp.max(jnp.where(sel, ms, -jnp.inf), axis=1, keepdims=True))
    w = jnp.where(sel, jnp.exp(jnp.where(sel, ms, mx) - mx), 0.0)
    wo = jnp.exp(mo - mx)
    den = wo * lo + jnp.sum(w * ls, axis=1, keepdims=True)
    num = wo * ao
    for jj in range(nbk):
        num = num + w[:, jj:jj + 1] * acc_ref[jj]
    res = num / den
    out = jnp.zeros((dec, da), F32)
    for h in range(nh):
        out = jnp.where(hl == h, res[h * dec:(h + 1) * dec, :], out)
    o_ref[...] = out


def _attn_sample(slopes, page_table, q, k, v, cache_kt, cache_vt, layer, bs, dec):
    da = q.shape[1]
    nh = da // HEAD_DIM
    n_pages = page_table.shape[1]
    past_len = n_pages * PAGE_SIZE
    blk = MOBA_BLOCK
    assert blk % PAGE_SIZE == 0 and past_len % blk == 0
    nbk = past_len // blk
    nb_total = -(-(past_len + dec) // blk)
    assert nb_total == nbk + 1 and nbk <= LANES
    nr = nh * dec
    new_rows = lambda b, pt: (b, 0)
    page_spec = lambda pg: pl.BlockSpec((None, None, nh, HEAD_DIM, PAGE_SIZE),
                                        lambda b, pt: (layer, pt[b, pg], 0, 0, 0))
    grid_spec = pltpu.PrefetchScalarGridSpec(
        num_scalar_prefetch=1,
        grid=(bs,),
        in_specs=[pl.BlockSpec(memory_space=pltpu.SMEM)] + [pl.BlockSpec((dec, da), new_rows)] * 3
                 + [page_spec(pg) for pg in range(n_pages)] * 2,
        out_specs=pl.BlockSpec((dec, da), lambda b, pt: (b, 0)),
        scratch_shapes=[pltpu.VMEM((nbk, nr, da), F32)],
    )
    return pl.pallas_call(
        functools.partial(_attn_sample_body, nh=nh, hd=HEAD_DIM, dec=dec, nbk=nbk, n_pages=n_pages, blk=blk,
                          n_sel=min(MOBA_TOPK, nb_total), past_len=past_len),
        grid_spec=grid_spec,
        out_shape=jax.ShapeDtypeStruct((bs * dec, da), F32),
        compiler_params=_params(1),
        name="attn_sample",
    )(page_table, slopes, q, k, v, *([cache_kt] * n_pages), *([cache_vt] * n_pages))


def _merge_body(xp_ref, xs_ref, cp_ref, cs_ref, ap_ref, as_ref, wg_ref, bg_ref, wco_ref, wao_ref, wo_ref, g1_ref, b1_ref,
                rw_ref, rb_ref, x1_ref, ridx_ref, rgate_ref, *, alpha, n_exp, ts, d, n_pt):
    is_prompt = pl.program_id(0) < n_pt
    c_pre = jnp.where(is_prompt, cp_ref[...], cs_ref[...])
    a_pre = jnp.where(is_prompt, ap_ref[...], as_ref[...])
    x = jnp.where(is_prompt, xp_ref[...], xs_ref[...])
    xb = x.astype(BF16)
    gc = jax.nn.sigmoid(_bdot(xb, wg_ref[:, 0:d]) + bg_ref[:, 0:d])
    ga = jax.nn.sigmoid(_bdot(xb, wg_ref[:, d:2 * d]) + bg_ref[:, d:2 * d])
    c = _bdot(c_pre.astype(BF16), wco_ref[...])
    a = _bdot(a_pre.astype(BF16), wao_ref[...])
    merged = gc * c + ga * a
    m = _bdot(merged.astype(BF16), wo_ref[...])
    x1 = _layer_norm(alpha * x + m, g1_ref[...], b1_ref[...])
    nch = d // LANES
    for ch in range(nch):
        x1_ref[pl.ds(ch, ts, stride=nch), :] = x1[:, ch * LANES:(ch + 1) * LANES]

    xh, xl = _split_bf16(x1)
    tot = _bdot(xh, rw_ref[...]) + _bdot(xl, rw_ref[...])
    lane = lax.broadcasted_iota(jnp.int32, (1, LANES), 1)
    logits = tot + pltpu.roll(tot, LANES - n_exp, axis=1) + rb_ref[...]
    logits = jnp.where(lane < n_exp, logits, -jnp.inf)
    vals, idxs = [], []
    for _ in range(TOP_K):
        mx = jnp.max(logits, axis=1, keepdims=True)
        ix = jnp.min(jnp.where(logits == mx, lane, LANES), axis=1, keepdims=True)
        vals.append(mx)
        idxs.append(ix)
        logits = jnp.where(lane == ix, -jnp.inf, logits)
    es = [jnp.exp(v - vals[0]) for v in vals]
    den = es[0]
    for e in es[1:]:
        den = den + e
    ridx = jnp.zeros((ts, LANES), jnp.int32)
    rgate = jnp.zeros((ts, LANES), F32)
    for kk in range(TOP_K):
        ridx = jnp.where(lane == kk, idxs[kk], ridx)
        rgate = jnp.where(lane == kk, es[kk] / den, rgate)
    ridx_ref[...] = ridx
    rgate_ref[...] = rgate


def _merge(x_p, x_s, c_p, c_s, a_p, a_s, wg, bg, wco, wao, wo, g1, b1, rw, rb, ts, alpha, n_exp):
    d = x_p.shape[1]
    t = x_p.shape[0] + x_s.shape[0]
    dc = c_p.shape[1]
    da = a_p.shape[1]
    n_pt = x_p.shape[0] // ts
    prow, srow = _split_maps(n_pt)
    nch = d // LANES
    row = lambda i: (i, 0)
    fixed = lambda i: (0, 0)
    full = lambda a: pl.BlockSpec(a.shape, fixed)
    return pl.pallas_call(
        functools.partial(_merge_body, alpha=alpha, n_exp=n_exp, ts=ts, d=d, n_pt=n_pt),
        grid=(t // ts,),
        in_specs=[pl.BlockSpec((ts, d), prow), pl.BlockSpec((ts, d), srow),
                  pl.BlockSpec((ts, dc), prow), pl.BlockSpec((ts, dc), srow),
                  pl.BlockSpec((ts, da), prow), pl.BlockSpec((ts, da), srow)]
                 + [full(a) for a in (wg, bg, wco, wao, wo, g1, b1, rw, rb)],
        out_specs=[pl.BlockSpec((ts * nch, LANES), row), pl.BlockSpec((ts, LANES), row), pl.BlockSpec((ts, LANES), row)],
        out_shape=[jax.ShapeDtypeStruct((t * nch, LANES), F32), jax.ShapeDtypeStruct((t, LANES), jnp.int32),
                   jax.ShapeDtypeStruct((t, LANES), F32)],
        compiler_params=_params(1),
        name="merge_router",
    )(x_p, x_s, c_p, c_s, a_p, a_s, wg, bg, wco, wao, wo, g1, b1, rw, rb)


def _route(ridx, n_exp, tm):
    t = ridx.shape[0]
    p0 = t * TOP_K
    assert p0 % tm == 0 and TOP_K & (TOP_K - 1) == 0
    n_tiles = p0 // tm
    n_items = n_tiles + n_exp - 1
    shift = (p0 - 1).bit_length()
    assert n_exp << shift < 2 ** 31
    i32 = lambda a: a.astype(jnp.int32)
    keys = (ridx[:, :TOP_K].reshape(p0) << shift) | jnp.arange(p0, dtype=jnp.int32)
    keys = jnp.sort(keys)
    order = keys & ((1 << shift) - 1)
    bounds = jnp.arange(n_exp + 1, dtype=jnp.int32) << shift
    first = i32(jnp.sum(i32(keys[None, :] < bounds[:, None]), axis=1))
    start, end = first[:-1], first[1:]
    n_items_e = jnp.where(end > start, (end - 1) // tm - start // tm + 1, 0)
    item_end = jnp.cumsum(n_items_e)
    item_start = item_end - n_items_e
    total = item_end[-1]
    ids = jnp.arange(n_items, dtype=jnp.int32)
    idc = jnp.minimum(ids, total - 1)
    it_exp = i32(jnp.sum(i32(item_end[None, :] <= idc[:, None]), axis=1))
    onehot = i32(it_exp[:, None] == jnp.arange(n_exp, dtype=jnp.int32)[None, :])
    pick = lambda table: jnp.sum(onehot * table[None, :], axis=1)
    it_start, it_end = pick(start), pick(end)
    it_tile = it_start // tm + (idc - pick(item_start))
    lo = jnp.maximum(it_start, it_tile * tm) - it_tile * tm
    hi = jnp.minimum(it_end, (it_tile + 1) * tm) - it_tile * tm
    valid = ids < total
    return (i32(it_tile), it_exp, i32(jnp.where(valid, lo, 0)), i32(jnp.where(valid, hi, 0)),
            i32(total).reshape(1), i32(order))


def _moe_body(tile_ref, exp_ref, lo_ref, hi_ref, total_ref, order_ref, x_hbm, wgu_ref, bgu_ref, wdn_ref, bdn_ref, ys_hbm,
              xbuf0, xbuf1, ybuf0, ybuf1, gsem, ssem, wgu_bf, wdn_bf, *, tm, d, de, nch, n_tiles, n_items):
    i = pl.program_id(0)
    total = total_ref[0]
    valid = i < total
    tile = tile_ref[i]
    prev = jnp.maximum(i - 1, 0)
    nxt = jnp.minimum(i + 1, n_items - 1)
    first = (i == 0) | (tile != tile_ref[prev])
    last = (i == total - 1) | (tile != tile_ref[nxt])
    new_expert = (i == 0) | (exp_ref[i] != exp_ref[prev])
    xbufs = (xbuf0, xbuf1)
    ybufs = (ybuf0, ybuf1)

    def gather_copy(pair, r, slot):
        tok = lax.shift_right_logical(pair, TOP_K.bit_length() - 1)
        return pltpu.make_async_copy(
            x_hbm.at[pl.ds(pl.multiple_of(tok * nch, nch), nch), :],
            xbufs[slot].at[pl.ds(pl.multiple_of(r * nch, nch), nch), :],
            gsem.at[slot])

    def scatter_copy(pair, r, slot):
        return pltpu.make_async_copy(
            ybufs[slot].at[pl.ds(pl.multiple_of(r * nch, nch), nch), :],
            ys_hbm.at[pl.ds(pl.multiple_of(pair * nch, nch), nch), :],
            ssem.at[slot])

    def issue(copy, t, slot):
        def body(g, carry):
            for u in range(DMA_ISSUE_UNROLL):
                r = g * DMA_ISSUE_UNROLL + u
                copy(order_ref[t * tm + r], r, slot).start(priority=u % 2)
            return carry
        lax.fori_loop(0, tm // DMA_ISSUE_UNROLL, body, 0)

    def wait_all(copy, slot):
        if copy is gather_copy:
            whole = pltpu.make_async_copy(x_hbm.at[pl.ds(0, tm * nch), :], xbufs[slot], gsem.at[slot])
        else:
            whole = pltpu.make_async_copy(ybufs[slot], ys_hbm.at[pl.ds(0, tm * nch), :], ssem.at[slot])
        whole.wait()

    @pl.when(i == 0)
    def _():
        issue(gather_copy, 0, 0)

    @pl.when(valid & new_expert)
    def _():
        rchunk = 64
        def body(c, carry):
            r0 = pl.multiple_of(c * rchunk, rchunk)
            wgu_bf[pl.ds(r0, rchunk), :] = wgu_ref[pl.ds(r0, rchunk), :].astype(BF16)
            return carry
        lax.fori_loop(0, d // rchunk, body, 0)
        def body2(c, carry):
            r0 = pl.multiple_of(c * rchunk, rchunk)
            wdn_bf[pl.ds(r0, rchunk), :] = wdn_ref[pl.ds(r0, rchunk), :].astype(BF16)
            return carry
        lax.fori_loop(0, de // rchunk, body2, 0)

    for slot in (0, 1):
        @pl.when(valid & (tile % 2 == slot))
        def _(slot=slot):
            @pl.when(first)
            def _():
                @pl.when(tile + 1 < n_tiles)
                def _():
                    issue(gather_copy, tile + 1, 1 - slot)
                wait_all(gather_copy, slot)
                @pl.when(tile >= 2)
                def _():
                    wait_all(scatter_copy, slot)

            xbuf, ybuf = xbufs[slot], ybufs[slot]
            x = jnp.concatenate([xbuf[pl.ds(ch, tm, stride=nch), :] for ch in range(nch)], axis=1).astype(BF16)
            h = _bdot(x, wgu_bf[...]) + bgu_ref[...]
            gate = jnp.minimum(h[:, 0:de], SWIGLU_LIMIT)
            up = jnp.clip(h[:, de:2 * de], -SWIGLU_LIMIT, SWIGLU_LIMIT)
            act = (up + 1.0) * (gate * jax.nn.sigmoid(SWIGLU_ALPHA * gate))
            y = _bdot(act.astype(BF16), wdn_bf[...]) + bdn_ref[...]
            rows = lax.broadcasted_iota(jnp.int32, (tm, 1), 0)
            y = jnp.where((rows >= lo_ref[i]) & (rows < hi_ref[i]), y, 0.0)

            @pl.when(first)
            def _():
                for ch in range(nch):
                    ybuf[pl.ds(ch, tm, stride=nch), :] = y[:, ch * LANES:(ch + 1) * LANES]

            @pl.when(jnp.logical_not(first))
            def _():
                for ch in range(nch):
                    ybuf[pl.ds(ch, tm, stride=nch), :] = ybuf[pl.ds(ch, tm, stride=nch), :] + y[:, ch * LANES:(ch + 1) * LANES]

            @pl.when(last)
            def _():
                issue(scatter_copy, tile, slot)

    @pl.when(i == total - 1)
    def _():
        for tl in range(max(n_tiles - 2, 0), n_tiles):
            wait_all(scatter_copy, tl % 2)


def _moe(it_tile, it_exp, it_lo, it_hi, total, order, x1t, w_gu, b_gu, w_dn, b_dn, layer, tm, d):
    n_items = it_tile.shape[0]
    p0 = order.shape[0]
    n_tiles = p0 // tm
    de = w_dn.shape[2]
    nch = d // LANES
    wmap = lambda i, tl, ex, lo, hi, tot, od: (layer, ex[i], 0, 0)
    grid_spec = pltpu.PrefetchScalarGridSpec(
        num_scalar_prefetch=6,
        grid=(n_items,),
        in_specs=[pl.BlockSpec(memory_space=pl.ANY),
                  pl.BlockSpec((None, None, d, 2 * de), wmap), pl.BlockSpec((None, None, 1, 2 * de), wmap),
                  pl.BlockSpec((None, None, de, d), wmap), pl.BlockSpec((None, None, 1, d), wmap)],
        out_specs=pl.BlockSpec(memory_space=pl.ANY),
        scratch_shapes=[pltpu.VMEM((tm * nch, LANES), F32)] * 4 + [pltpu.SemaphoreType.DMA((2,))] * 2
                       + [pltpu.VMEM((d, 2 * de), BF16), pltpu.VMEM((de, d), BF16)],
    )
    return pl.pallas_call(
        functools.partial(_moe_body, tm=tm, d=d, de=de, nch=nch, n_tiles=n_tiles, n_items=n_items),
        grid_spec=grid_spec,
        out_shape=jax.ShapeDtypeStruct((p0 * nch, LANES), F32),
        compiler_params=_params(1),
        name="moe_experts",
    )(it_tile, it_exp, it_lo, it_hi, total, order, x1t, w_gu, b_gu, w_dn, b_dn)


def _combine_body(ys_ref, rg_ref, x1_ref, pp_ref, ps_ref, g2_ref, b2_ref, wpg_ref, bpg_ref, wp_ref, g3_ref, b3_ref,
                  op_ref, os_ref, *, tf, nch, alpha, n_pt):
    is_prompt = pl.program_id(0) < n_pt
    ple = jnp.where(is_prompt, pp_ref[...], ps_ref[...])
    rg = rg_ref[...]
    gk = [jnp.broadcast_to(rg[:, kk:kk + 1], (tf, LANES)) for kk in range(TOP_K)]
    stride = TOP_K * nch
    fs, xs = [], []
    for ch in range(nch):
        f = gk[0] * ys_ref[pl.ds(ch, tf, stride=stride), :]
        for kk in range(1, TOP_K):
            f = f + gk[kk] * ys_ref[pl.ds(kk * nch + ch, tf, stride=stride), :]
        fs.append(f)
        xs.append(x1_ref[pl.ds(ch, tf, stride=nch), :])
    f = jnp.concatenate(fs, axis=1)
    x1 = jnp.concatenate(xs, axis=1)
    x2 = _layer_norm(alpha * x1 + f, g2_ref[...], b2_ref[...])
    e = jax.nn.sigmoid(_bdot(x2.astype(BF16), wpg_ref[...]) + bpg_ref[...]) * _bdot(ple.astype(BF16), wp_ref[...])
    x3 = _layer_norm(alpha * x2 + e, g3_ref[...], b3_ref[...])

    @pl.when(is_prompt)
    def _():
        op_ref[...] = x3

    @pl.when(jnp.logical_not(is_prompt))
    def _():
        os_ref[...] = x3


def _combine(ys, rgate, x1t, p_p, p_s, g2, b2, wpg, bpg, wp, g3, b3, tf, d, alpha):
    tp, tsmp = p_p.shape[0], p_s.shape[0]
    nch = d // LANES
    n_pt = tp // tf
    prow, srow = _split_maps(n_pt)
    row = lambda i: (i, 0)
    fixed = lambda i: (0, 0)
    full = lambda a: pl.BlockSpec(a.shape, fixed)
    return pl.pallas_call(
        functools.partial(_combine_body, tf=tf, nch=nch, alpha=alpha, n_pt=n_pt),
        grid=((tp + tsmp) // tf,),
        in_specs=[pl.BlockSpec((tf * TOP_K * nch, LANES), row), pl.BlockSpec((tf, LANES), row),
                  pl.BlockSpec((tf * nch, LANES), row), pl.BlockSpec((tf, p_p.shape[1]), prow),
                  pl.BlockSpec((tf, p_s.shape[1]), srow)]
                 + [full(a) for a in (g2, b2, wpg, bpg, wp, g3, b3)],
        out_specs=[pl.BlockSpec((tf, d), prow), pl.BlockSpec((tf, d), srow)],
        out_shape=[jax.ShapeDtypeStruct((tp, d), F32), jax.ShapeDtypeStruct((tsmp, d), F32)],
        compiler_params=_params(1),
        name="combine_ple",
    )(ys, rgate, x1t, p_p, p_s, g2, b2, wpg, bpg, wp, g3, b3)


def kernel(x_prompt, x_sample, p_prompt, p_sample, cache_k, cache_v, state_conv, page_table, w_in, b_in, conv_w, conv_b, conv_ln_g, conv_ln_b, w_conv_out, w_attn_out, w_out, ln1_g, ln1_b, router_w, router_b, exp_w_gu, exp_b_gu, exp_w_down, exp_b_down, ln2_g, ln2_b, ple_w, ple_gate_w, ple_gate_b, ln3_g, ln3_b):
    bp, sp, d = x_prompt.shape
    bs, dec, _ = x_sample.shape
    depth = w_in.shape[0]
    d_conv = conv_w.shape[2]
    d_attn = w_attn_out.shape[1]
    nh = d_attn // HEAD_DIM
    n_exp = router_w.shape[2]
    tp, tsmp = bp * sp, bs * dec
    alpha = (2 * depth) ** 0.25
    assert 2 * n_exp <= LANES and d % LANES == 0

    ts = _pick_tile(512, tp, tsmp)
    tf = _pick_tile(256, tp, tsmp)
    tm = 256
    row2 = lambda a: a.reshape(1, -1)

    x_p, x_s = x_prompt.reshape(tp, d), x_sample.reshape(tsmp, d)
    slopes = jnp.asarray([2.0 ** (-8.0 * (h + 1) / nh) for h in range(nh)], dtype=F32)
    cache_kt = jnp.transpose(cache_k, (0, 1, 3, 4, 2))
    cache_vt = jnp.transpose(cache_v, (0, 1, 3, 4, 2))
    b_gu4 = exp_b_gu[:, :, None, :]
    b_dn4 = exp_b_down[:, :, None, :]

    k_p, v_p, c_p, k_s, v_s, c_s = [], [], [], [], [], []
    n_qkv = 2 * d_conv + 3 * d_attn
    for l in range(depth):
        u_p, q_p, k_pl, v_pl, u_s, q_s, k_sl, v_sl = _inproj(
            x_p, x_s, w_in[l][:, :n_qkv].astype(BF16), row2(b_in[l][:n_qkv]), ts, d_conv, d_attn)

        conv_args = (conv_w[l], row2(conv_b[l]), row2(conv_ln_g[l]), row2(conv_ln_b[l]))
        c_p_pre = _conv_prompt(u_p, *conv_args, bp, sp)
        u_s3 = u_s.reshape(bs, dec, d_conv)
        c_s_t = _conv_sample(jnp.transpose(u_s3, (1, 0, 2)), jnp.transpose(state_conv[l], (1, 0, 2)), *conv_args)
        c_s_pre = jnp.transpose(c_s_t, (1, 0, 2)).reshape(tsmp, d_conv)

        a_p_pre = _attn_prompt(slopes, q_p, k_pl, v_pl, bp, sp)
        a_s_pre = _attn_sample(slopes, page_table, q_s, k_sl, v_sl, cache_kt, cache_vt, l, bs, dec)

        rw_hi, rw_lo = _split_bf16(router_w[l])
        rw = jnp.zeros((d, LANES), BF16).at[:, :n_exp].set(rw_hi).at[:, n_exp:2 * n_exp].set(rw_lo)
        rb = jnp.zeros((1, LANES), F32).at[0, :n_exp].set(router_b[l])
        x1t, ridx, rgate = _merge(
            x_p, x_s, c_p_pre, c_s_pre, a_p_pre, a_s_pre, w_in[l][:, n_qkv:].astype(BF16), row2(b_in[l][n_qkv:]),
            w_conv_out[l].astype(BF16), w_attn_out[l].astype(BF16), w_out[l].astype(BF16),
            row2(ln1_g[l]), row2(ln1_b[l]), rw, rb, ts, alpha, n_exp)

        ys = _moe(*_route(ridx, n_exp, tm), x1t, exp_w_gu, b_gu4, exp_w_down, b_dn4, l, tm, d)
        x_p, x_s = _combine(
            ys, rgate, x1t, p_prompt[l].reshape(tp, -1), p_sample[l].reshape(tsmp, -1), row2(ln2_g[l]), row2(ln2_b[l]),
            ple_gate_w[l].astype(BF16), row2(ple_gate_b[l]), ple_w[l].astype(BF16), row2(ln3_g[l]), row2(ln3_b[l]),
            tf, d, alpha)

        hist = conv_w.shape[1] - 1
        k_p.append(k_pl.reshape(bp, sp, nh, HEAD_DIM))
        v_p.append(v_pl.reshape(bp, sp, nh, HEAD_DIM))
        c_p.append(u_p.reshape(bp, sp, d_conv)[:, sp - hist:, :])
        k_s.append(k_sl.reshape(bs, dec, nh, HEAD_DIM))
        v_s.append(v_sl.reshape(bs, dec, nh, HEAD_DIM))
        c_s.append(jnp.concatenate([state_conv[l], u_s3], axis=1)[:, -hist:, :])

    return (x_p.reshape(bp, sp, d), x_s.reshape(bs, dec, d), jnp.stack(k_p), jnp.stack(v_p), jnp.stack(c_p),
            jnp.stack(k_s), jnp.stack(v_s), jnp.stack(c_s))
```

```python
import functools

import jax
import jax.numpy as jnp
from jax import lax
from jax.experimental import pallas as pl
from jax.experimental.pallas import tpu as pltpu

F32 = jnp.float32
BF16 = jnp.bfloat16

HEAD_DIM = 64
MOBA_BLOCK = 256
MOBA_TOPK = 3
PAGE_SIZE = 128
TOP_K = 4
SWIGLU_ALPHA = 1.702
SWIGLU_LIMIT = 7.0
LN_EPS = 1e-5

LANES = 128
SUBLANES = 8
VMEM_LIMIT_BYTES = 56 * 1024 * 1024
DMA_ISSUE_UNROLL = 8

NT_DIMS = (((1,), (1,)), ((), ()))


def _params(n_axes, vmem=None):
    return pltpu.CompilerParams(
        dimension_semantics=("arbitrary",) * n_axes,
        vmem_limit_bytes=VMEM_LIMIT_BYTES if vmem is None else vmem,
    )


def _pick_tile(cap, *sizes):
    t = cap
    while any(s % t for s in sizes):
        t //= 2
    return t


def _layer_norm(x, g, b):
    mu = jnp.mean(x, axis=-1, keepdims=True)
    xc = x - mu
    var = jnp.mean(xc * xc, axis=-1, keepdims=True)
    return xc * lax.rsqrt(var + LN_EPS) * g + b


def _split_bf16(a):
    hi = a.astype(BF16)
    lo = (a - hi.astype(F32)).astype(BF16)
    return hi, lo


def _dot3_nt(a, b):
    ah, al = _split_bf16(a)
    bh, bl = _split_bf16(b)
    f = lambda x, y: lax.dot_general(x, y, NT_DIMS, preferred_element_type=F32)
    return f(ah, bh) + (f(ah, bl) + f(al, bh))


def _bdot(a, b):
    return jnp.dot(a, b, preferred_element_type=F32)


def _split_maps(n_pt):
    return (lambda i: (jnp.minimum(i, n_pt - 1), 0)), (lambda i: (jnp.maximum(i - n_pt, 0), 0))


def _inproj_body(xp_ref, xs_ref, w_ref, b_ref, *out_refs, d_conv, d_attn, n_pt):
    is_prompt = pl.program_id(0) < n_pt
    x = jnp.where(is_prompt, xp_ref[...], xs_ref[...]).astype(BF16)

    def proj(lo, n):
        return _bdot(x, w_ref[:, lo:lo + n]) + b_ref[:, lo:lo + n]

    o = 2 * d_conv
    vals = (proj(0, d_conv) * jax.nn.sigmoid(proj(d_conv, d_conv)),
            proj(o, d_attn), proj(o + d_attn, d_attn), proj(o + 2 * d_attn, d_attn))

    @pl.when(is_prompt)
    def _():
        for ref, val in zip(out_refs[0:4], vals):
            ref[...] = val

    @pl.when(jnp.logical_not(is_prompt))
    def _():
        for ref, val in zip(out_refs[4:8], vals):
            ref[...] = val


def _inproj(x_p, x_s, w_bf, b, ts, d_conv, d_attn):
    tp, d = x_p.shape
    tsmp = x_s.shape[0]
    n = w_bf.shape[1]
    n_pt = tp // ts
    prow, srow = _split_maps(n_pt)
    fixed = lambda i: (0, 0)
    widths = (d_conv, d_attn, d_attn, d_attn)
    return pl.pallas_call(
        functools.partial(_inproj_body, d_conv=d_conv, d_attn=d_attn, n_pt=n_pt),
        grid=((tp + tsmp) // ts,),
        in_specs=[pl.BlockSpec((ts, d), prow), pl.BlockSpec((ts, d), srow), pl.BlockSpec((d, n), fixed),
                  pl.BlockSpec((1, n), fixed)],
        out_specs=[pl.BlockSpec((ts, w), prow) for w in widths] + [pl.BlockSpec((ts, w), srow) for w in widths],
        out_shape=[jax.ShapeDtypeStruct((tp, w), F32) for w in widths]
                  + [jax.ShapeDtypeStruct((tsmp, w), F32) for w in widths],
        compiler_params=_params(1),
        name="inproj",
    )(x_p, x_s, w_bf, b)


def _conv_prompt_body(u_ref, w_ref, cb_ref, g_ref, b_ref, o_ref, win_ref, sh_ref, *, ts, width, halo, rc):
    s = pl.program_id(1)
    c = u_ref.shape[1]
    span = halo + ts

    @pl.when(s == 0)
    def _():
        win_ref[0:halo, :] = jnp.zeros((halo, c), F32)

    @pl.when(s > 0)
    def _():
        win_ref[0:halo, :] = win_ref[ts:ts + halo, :]

    win_ref[halo:span, :] = u_ref[...]
    win_ref[span:span + SUBLANES, :] = jnp.zeros((SUBLANES, c), F32)
    for ph in range(1, SUBLANES):
        sh_ref[ph - 1] = win_ref[pl.ds(ph, span), :]
    off = halo - (width - 1)
    for r0 in range(0, ts, rc):
        acc = jnp.broadcast_to(cb_ref[...], (rc, c))
        for j in range(width):
            ph = (off + j) % SUBLANES
            base = r0 + off + j - ph
            rows = win_ref[base:base + rc, :] if ph == 0 else sh_ref[ph - 1, base:base + rc, :]
            acc = acc + rows * w_ref[j:j + 1, :]
        y = _layer_norm(acc, g_ref[...], b_ref[...])
        o_ref[r0:r0 + rc, :] = y * jax.nn.sigmoid(y)


def _conv_prompt(u, conv_w, conv_b, ln_g, ln_b, bp, sp):
    c = u.shape[1]
    width = conv_w.shape[0]
    halo = -(-(width - 1) // SUBLANES) * SUBLANES
    ts = _pick_tile(256, sp)
    assert ts >= halo
    ns = sp // ts
    tile = lambda b, s: (b * ns + s, 0)
    fixed = lambda b, s: (0, 0)
    return pl.pallas_call(
        functools.partial(_conv_prompt_body, ts=ts, width=width, halo=halo, rc=min(64, ts)),
        grid=(bp, ns),
        in_specs=[pl.BlockSpec((ts, c), tile), pl.BlockSpec((width, c), fixed)] + [pl.BlockSpec((1, c), fixed)] * 3,
        out_specs=pl.BlockSpec((ts, c), tile),
        out_shape=jax.ShapeDtypeStruct((bp * sp, c), F32),
        scratch_shapes=[pltpu.VMEM((halo + ts + SUBLANES, c), F32), pltpu.VMEM((SUBLANES - 1, halo + ts, c), F32)],
        compiler_params=_params(2),
        name="conv_prompt",
    )(u, conv_w, conv_b, ln_g, ln_b)


def _conv_sample_body(u_ref, h_ref, w_ref, cb_ref, g_ref, b_ref, o_ref, *, nb, dec, width):
    c = u_ref.shape[2]
    hist = width - 1

    def full_row(r):
        return h_ref[r] if r < hist else u_ref[r - hist]

    for t in range(dec):
        acc = jnp.broadcast_to(cb_ref[...], (nb, c))
        for j in range(width):
            acc = acc + full_row(t + j) * w_ref[j:j + 1, :]
        y = _layer_norm(acc, g_ref[...], b_ref[...])
        o_ref[t] = y * jax.nn.sigmoid(y)


def _conv_sample(u_t, hist_t, conv_w, conv_b, ln_g, ln_b):
    dec, bs, c = u_t.shape
    width = conv_w.shape[0]
    nb = min(32, bs)
    assert bs % nb == 0
    blockg = lambda g: (0, g, 0)
    fixed = lambda g: (0, 0)
    return pl.pallas_call(
        functools.partial(_conv_sample_body, nb=nb, dec=dec, width=width),
        grid=(bs // nb,),
        in_specs=[pl.BlockSpec((dec, nb, c), blockg), pl.BlockSpec((width - 1, nb, c), blockg),
                  pl.BlockSpec((width, c), fixed)] + [pl.BlockSpec((1, c), fixed)] * 3,
        out_specs=pl.BlockSpec((dec, nb, c), blockg),
        out_shape=jax.ShapeDtypeStruct((dec, bs, c), F32),
        compiler_params=_params(1),
        name="conv_sample",
    )(u_t, hist_t, conv_w, conv_b, ln_g, ln_b)


def _attn_prompt_body(slopes_ref, q_ref, k_ref, v_ref, o_ref, kmean_ref, bias_ref, *, blk, nb, n_sel, hd):
    hp = pl.program_id(1)
    qi = pl.program_id(2)

    nhl = LANES // hd
    rows = nhl * blk
    lane = lax.broadcasted_iota(jnp.int32, (1, LANES), 1)
    rowi = lax.broadcasted_iota(jnp.int32, (rows, 1), 0)
    rhead = rowi // blk
    rq = rowi % blk
    col = lax.broadcasted_iota(jnp.int32, (1, blk), 1)
    scale = HEAD_DIM ** -0.5
    neg_inf = -jnp.inf
    slope = jnp.zeros((rows, 1), F32)
    for hh in range(nhl):
        slope = jnp.where(rhead == hh, slopes_ref[hp * nhl + hh], slope)

    @pl.when(qi == 0)
    def _():
        kmean_ref[...] = jnp.zeros(kmean_ref.shape, F32)
        for j in range(nb):
            kmean_ref[j:j + 1, :] = jnp.mean(k_ref[j * blk:(j + 1) * blk, :], axis=0, keepdims=True)
        bias_ref[...] = slope * (rq - col).astype(F32)

    q = q_ref[...]
    q2 = jnp.where(lane // hd == rhead, jnp.concatenate([q] * nhl, axis=0), 0.0)
    q2b = (q2 * scale).astype(BF16)

    nbp = 2 * SUBLANES
    gate = _dot3_nt(kmean_ref[0:nbp, :], q2)
    jrow = lax.broadcasted_iota(jnp.int32, (nbp, 1), 0)
    rank = jnp.zeros((nbp, rows), F32)
    for jj in range(nb):
        gj = gate[jj:jj + 1, :]
        beats = (jj < qi) & ((gj > gate) | ((gj == gate) & (jj < jrow)))
        rank = rank + beats.astype(F32)
    sel_t = jnp.where((rank < n_sel) & (jrow < qi), 1.0, 0.0).astype(BF16)
    eye = jnp.where(jrow == lane, 1.0, 0.0).astype(BF16)
    sel = lax.dot_general(sel_t, eye, (((0,), (0,)), ((), ())), preferred_element_type=F32)

    def scores(j):
        kb = k_ref[pl.ds(pl.multiple_of(j * blk, blk), blk), :].astype(BF16)
        return lax.dot_general(q2b, kb, NT_DIMS, preferred_element_type=F32) - bias_ref[...]

    def pv(p, j):
        vb = v_ref[pl.ds(pl.multiple_of(j * blk, blk), blk), :].astype(BF16)
        return _bdot(p.astype(BF16), vb)

    s = jnp.where(rq >= col, scores(qi), neg_inf)
    m = jnp.max(s, axis=1, keepdims=True)
    p = jnp.exp(s - m)
    l = jnp.sum(p, axis=1, keepdims=True)
    acc = pv(p, qi)

    def body(j, carry):
        m, l, acc = carry
        pk = jnp.sum(jnp.where(lane == j, sel, 0.0), axis=1, keepdims=True) > 0.5
        off = slope * ((qi - j) * blk).astype(F32)
        s = scores(j)
        m_new = jnp.maximum(m, jnp.where(pk, jnp.max(s, axis=1, keepdims=True) - off, neg_inf))
        a = jnp.exp(m - m_new)
        p = jnp.exp(s - jnp.where(pk, m_new + off, jnp.inf))
        return m_new, a * l + jnp.sum(p, axis=1, keepdims=True), a * acc + pv(p, j)

    m, l, acc = lax.fori_loop(0, qi, body, (m, l, acc))
    res = acc / l
    o = res[0:blk, :]
    for hh in range(1, nhl):
        o = jnp.where(lane // hd == hh, res[hh * blk:(hh + 1) * blk, :], o)
    o_ref[...] = o


def _attn_prompt(slopes, q, k, v, bp, sp):
    da = q.shape[1]
    blk = MOBA_BLOCK
    nb = sp // blk
    assert sp % blk == 0 and nb <= 2 * SUBLANES and da % LANES == 0
    qtile = lambda b, hp, qi: (b * nb + qi, hp)
    seq = lambda b, hp, qi: (b, hp)
    return pl.pallas_call(
        functools.partial(_attn_prompt_body, blk=blk, nb=nb, n_sel=min(MOBA_TOPK, nb), hd=HEAD_DIM),
        grid=(bp, da // LANES, nb),
        in_specs=[pl.BlockSpec(memory_space=pltpu.SMEM), pl.BlockSpec((blk, LANES), qtile),
                  pl.BlockSpec((sp, LANES), seq), pl.BlockSpec((sp, LANES), seq)],
        out_specs=pl.BlockSpec((blk, LANES), qtile),
        out_shape=jax.ShapeDtypeStruct((bp * sp, da), F32),
        scratch_shapes=[pltpu.VMEM((LANES, LANES), F32), pltpu.VMEM((LANES // HEAD_DIM * blk, blk), F32)],
        compiler_params=_params(3),
        name="attn_prompt",
    )(slopes, q, k, v)


def _attn_sample_body(pt_ref, slopes_ref, q_ref, kn_ref, vn_ref, *refs, nh, hd, dec, nbk, n_pages, blk, n_sel, past_len):
    del pt_ref
    k_refs = refs[0:n_pages]
    v_refs = refs[n_pages:2 * n_pages]
    o_ref = refs[2 * n_pages]
    acc_ref = refs[2 * n_pages + 1]
    da = nh * hd
    nr = nh * dec
    ppb = blk // PAGE_SIZE
    scale = HEAD_DIM ** -0.5
    lane = lax.broadcasted_iota(jnp.int32, (1, LANES), 1)
    rowi = lax.broadcasted_iota(jnp.int32, (nr, 1), 0)
    slope = jnp.zeros((nr, 1), F32)
    for h in range(nh):
        slope = jnp.where(rowi // dec == h, slopes_ref[h], slope)
    tq = rowi % dec

    q = q_ref[...]
    hl = lax.broadcasted_iota(jnp.int32, (1, da), 1) // hd
    qbd = jnp.where(rowi // dec == hl, jnp.concatenate([q] * nh, axis=0), 0.0)
    qh, ql = _split_bf16(qbd)
    q2 = jnp.concatenate([qh, ql], axis=0)

    cols = [jnp.sum(qbd * kn_ref[t:t + 1, :], axis=1, keepdims=True) for t in range(dec)]
    s = jnp.concatenate(cols, axis=1) * scale
    tk = lax.broadcasted_iota(jnp.int32, (1, dec), 1)
    dist = (tq - tk).astype(F32)
    s = jnp.where(dist >= 0, s - slope * dist, -jnp.inf)
    mo = jnp.max(s, axis=1, keepdims=True)
    p = jnp.exp(s - mo)
    lo = jnp.sum(p, axis=1, keepdims=True)
    ao = jnp.zeros((nr, da), F32)
    for t in range(dec):
        ao = ao + p[:, t:t + 1] * vn_ref[t:t + 1, :]

    g = jnp.zeros((nr, LANES), F32)
    ms = jnp.zeros((nr, LANES), F32)
    ls = jnp.zeros((nr, LANES), F32)
    def page_block(refs, j):
        return jnp.concatenate([refs[ppb * j + o][...].reshape(da, PAGE_SIZE) for o in range(ppb)], axis=1).astype(BF16)

    raws = []
    for j in range(nbk):
        s2 = _bdot(q2, page_block(k_refs, j))
        raws.append(s2[0:nr, :] + s2[nr:2 * nr, :])
    ps = []
    for j in range(nbk):
        gate = jnp.mean(raws[j], axis=1, keepdims=True)
        kpos = j * blk + lax.broadcasted_iota(jnp.int32, (1, blk), 1)
        dist = (past_len + tq - kpos).astype(F32)
        s = raws[j] * scale - slope * dist
        m = jnp.max(s, axis=1, keepdims=True)
        p = jnp.exp(s - m)
        ps.append(p.astype(BF16))
        g = jnp.where(lane == j, gate, g)
        ms = jnp.where(lane == j, m, ms)
        ls = jnp.where(lane == j, jnp.sum(p, axis=1, keepdims=True), ls)
    for j in range(nbk):
        acc_ref[j] = lax.dot_general(ps[j], page_block(v_refs, j), NT_DIMS, preferred_element_type=F32)

    rank = jnp.zeros((nr, LANES), F32)
    for jj in range(nbk):
        gj = g[:, jj:jj + 1]
        rank = rank + ((gj > g) | ((gj == g) & (jj < lane))).astype(F32)
    sel = (rank < n_sel) & (lane < nbk)
    mx = jnp.maximum(mo, jnp.max(jnp.where(sel, ms, -jnp.inf), axis=1, keepdims=True))
    w = jnp.where(sel, jnp.exp(jnp.where(sel, ms, mx) - mx), 0.0)
    wo = jnp.exp(mo - mx)
    den = wo * lo + jnp.sum(w * ls, axis=1, keepdims=True)
    num = wo * ao
    for jj in range(nbk):
        num = num + w[:, jj:jj + 1] * acc_ref[jj]
    res = num / den
    out = jnp.zeros((dec, da), F32)
    for h in range(nh):
        out = jnp.where(hl == h, res[h * dec:(h + 1) * dec, :], out)
    o_ref[...] = out


def _attn_sample(slopes, page_table, q, k, v, cache_kt, cache_vt, layer, bs, dec):
    da = q.shape[1]
    nh = da // HEAD_DIM
    n_pages = page_table.shape[1]
    past_len = n_pages * PAGE_SIZE
    blk = MOBA_BLOCK
    assert blk % PAGE_SIZE == 0 and past_len % blk == 0
    nbk = past_len // blk
    nb_total = -(-(past_len + dec) // blk)
    assert nb_total == nbk + 1 and nbk <= LANES
    nr = nh * dec
    new_rows = lambda b, pt: (b, 0)
    page_spec = lambda pg: pl.BlockSpec((None, None, nh, HEAD_DIM, PAGE_SIZE),
                                        lambda b, pt: (layer, pt[b, pg], 0, 0, 0))
    grid_spec = pltpu.PrefetchScalarGridSpec(
        num_scalar_prefetch=1,
        grid=(bs,),
        in_specs=[pl.BlockSpec(memory_space=pltpu.SMEM)] + [pl.BlockSpec((dec, da), new_rows)] * 3
                 + [page_spec(pg) for pg in range(n_pages)] * 2,
        out_specs=pl.BlockSpec((dec, da), lambda b, pt: (b, 0)),
        scratch_shapes=[pltpu.VMEM((nbk, nr, da), F32)],
    )
    return pl.pallas_call(
        functools.partial(_attn_sample_body, nh=nh, hd=HEAD_DIM, dec=dec, nbk=nbk, n_pages=n_pages, blk=blk,
                          n_sel=min(MOBA_TOPK, nb_total), past_len=past_len),
        grid_spec=grid_spec,
        out_shape=jax.ShapeDtypeStruct((bs * dec, da), F32),
        compiler_params=_params(1),
        name="attn_sample",
    )(page_table, slopes, q, k, v, *([cache_kt] * n_pages), *([cache_vt] * n_pages))


def _merge_body(xp_ref, xs_ref, cp_ref, cs_ref, ap_ref, as_ref, wg_ref, bg_ref, wco_ref, wao_ref, wo_ref, g1_ref, b1_ref,
                rw_ref, rb_ref, x1_ref, ridx_ref, rgate_ref, *, alpha, n_exp, ts, d, n_pt):
    is_prompt = pl.program_id(0) < n_pt
    c_pre = jnp.where(is_prompt, cp_ref[...], cs_ref[...])
    a_pre = jnp.where(is_prompt, ap_ref[...], as_ref[...])
    x = jnp.where(is_prompt, xp_ref[...], xs_ref[...])
    xb = x.astype(BF16)
    gc = jax.nn.sigmoid(_bdot(xb, wg_ref[:, 0:d]) + bg_ref[:, 0:d])
    ga = jax.nn.sigmoid(_bdot(xb, wg_ref[:, d:2 * d]) + bg_ref[:, d:2 * d])
    c = _bdot(c_pre.astype(BF16), wco_ref[...])
    a = _bdot(a_pre.astype(BF16), wao_ref[...])
    merged = gc * c + ga * a
    m = _bdot(merged.astype(BF16), wo_ref[...])
    x1 = _layer_norm(alpha * x + m, g1_ref[...], b1_ref[...])
    nch = d // LANES
    for ch in range(nch):
        x1_ref[pl.ds(ch, ts, stride=nch), :] = x1[:, ch * LANES:(ch + 1) * LANES]

    xh, xl = _split_bf16(x1)
    tot = _bdot(xh, rw_ref[...]) + _bdot(xl, rw_ref[...])
    lane = lax.broadcasted_iota(jnp.int32, (1, LANES), 1)
    logits = tot + pltpu.roll(tot, LANES - n_exp, axis=1) + rb_ref[...]
    logits = jnp.where(lane < n_exp, logits, -jnp.inf)
    vals, idxs = [], []
    for _ in range(TOP_K):
        mx = jnp.max(logits, axis=1, keepdims=True)
        ix = jnp.min(jnp.where(logits == mx, lane, LANES), axis=1, keepdims=True)
        vals.append(mx)
        idxs.append(ix)
        logits = jnp.where(lane == ix, -jnp.inf, logits)
    es = [jnp.exp(v - vals[0]) for v in vals]
    den = es[0]
    for e in es[1:]:
        den = den + e
    ridx = jnp.zeros((ts, LANES), jnp.int32)
    rgate = jnp.zeros((ts, LANES), F32)
    for kk in range(TOP_K):
        ridx = jnp.where(lane == kk, idxs[kk], ridx)
        rgate = jnp.where(lane == kk, es[kk] / den, rgate)
    ridx_ref[...] = ridx
    rgate_ref[...] = rgate


def _merge(x_p, x_s, c_p, c_s, a_p, a_s, wg, bg, wco, wao, wo, g1, b1, rw, rb, ts, alpha, n_exp):
    d = x_p.shape[1]
    t = x_p.shape[0] + x_s.shape[0]
    dc = c_p.shape[1]
    da = a_p.shape[1]
    n_pt = x_p.shape[0] // ts
    prow, srow = _split_maps(n_pt)
    nch = d // LANES
    row = lambda i: (i, 0)
    fixed = lambda i: (0, 0)
    full = lambda a: pl.BlockSpec(a.shape, fixed)
    return pl.pallas_call(
        functools.partial(_merge_body, alpha=alpha, n_exp=n_exp, ts=ts, d=d, n_pt=n_pt),
        grid=(t // ts,),
        in_specs=[pl.BlockSpec((ts, d), prow), pl.BlockSpec((ts, d), srow),
                  pl.BlockSpec((ts, dc), prow), pl.BlockSpec((ts, dc), srow),
                  pl.BlockSpec((ts, da), prow), pl.BlockSpec((ts, da), srow)]
                 + [full(a) for a in (wg, bg, wco, wao, wo, g1, b1, rw, rb)],
        out_specs=[pl.BlockSpec((ts * nch, LANES), row), pl.BlockSpec((ts, LANES), row), pl.BlockSpec((ts, LANES), row)],
        out_shape=[jax.ShapeDtypeStruct((t * nch, LANES), F32), jax.ShapeDtypeStruct((t, LANES), jnp.int32),
                   jax.ShapeDtypeStruct((t, LANES), F32)],
        compiler_params=_params(1),
        name="merge_router",
    )(x_p, x_s, c_p, c_s, a_p, a_s, wg, bg, wco, wao, wo, g1, b1, rw, rb)


def _tok_bits(t):
    return (t - 1).bit_length()


def _route(ridx, n_exp, tm, tf):
    t = ridx.shape[0]
    p0 = t * TOP_K
    assert p0 % tm == 0 and TOP_K & (TOP_K - 1) == 0 and t % tf == 0
    n_tiles = p0 // tm
    n_items = n_tiles + n_exp - 1
    shift = (p0 - 1).bit_length()
    tok_bits = _tok_bits(t)
    assert n_exp << shift < 2 ** 31 and shift + tok_bits <= 32
    i32 = lambda a: a.astype(jnp.int32)
    keys = (ridx[:, :TOP_K].reshape(p0) << shift) | jnp.arange(p0, dtype=jnp.int32)
    keys = jnp.sort(keys)
    pair = keys & ((1 << shift) - 1)
    tok = pair // TOP_K
    dst = ((tok // tf) * TOP_K + pair % TOP_K) * tf + tok % tf
    order = lax.bitcast_convert_type((dst.astype(jnp.uint32) << tok_bits) | tok.astype(jnp.uint32), jnp.int32)
    bounds = jnp.arange(n_exp + 1, dtype=jnp.int32) << shift
    first = i32(jnp.sum(i32(keys[None, :] < bounds[:, None]), axis=1))
    start, end = first[:-1], first[1:]
    n_items_e = jnp.where(end > start, (end - 1) // tm - start // tm + 1, 0)
    item_end = jnp.cumsum(n_items_e)
    item_start = item_end - n_items_e
    total = item_end[-1]
    ids = jnp.arange(n_items, dtype=jnp.int32)
    idc = jnp.minimum(ids, total - 1)
    it_exp = i32(jnp.sum(i32(item_end[None, :] <= idc[:, None]), axis=1))
    onehot = i32(it_exp[:, None] == jnp.arange(n_exp, dtype=jnp.int32)[None, :])
    pick = lambda table: jnp.sum(onehot * table[None, :], axis=1)
    it_start, it_end = pick(start), pick(end)
    it_tile = it_start // tm + (idc - pick(item_start))
    lo = jnp.maximum(it_start, it_tile * tm) - it_tile * tm
    hi = jnp.minimum(it_end, (it_tile + 1) * tm) - it_tile * tm
    valid = ids < total
    return (i32(it_tile), it_exp, i32(jnp.where(valid, lo, 0)), i32(jnp.where(valid, hi, 0)),
            i32(total).reshape(1), i32(order))


def _moe_body(tile_ref, exp_ref, lo_ref, hi_ref, total_ref, order_ref, x_hbm, wgu_ref, bgu_ref, wdn_ref, bdn_ref, ys_hbm,
              xbuf0, xbuf1, ybuf0, ybuf1, gsem, ssem, wgu_bf, wdn_bf, *, tm, d, de, nch, n_tiles, n_items, tok_bits):
    i = pl.program_id(0)
    total = total_ref[0]
    valid = i < total
    tile = tile_ref[i]
    prev = jnp.maximum(i - 1, 0)
    nxt = jnp.minimum(i + 1, n_items - 1)
    first = (i == 0) | (tile != tile_ref[prev])
    last = (i == total - 1) | (tile != tile_ref[nxt])
    new_expert = (i == 0) | (exp_ref[i] != exp_ref[prev])
    xbufs = (xbuf0, xbuf1)
    ybufs = (ybuf0, ybuf1)

    def gather_copy(word, r, slot):
        tok = word & ((1 << tok_bits) - 1)
        return pltpu.make_async_copy(
            x_hbm.at[pl.ds(pl.multiple_of(tok * nch, nch), nch), :],
            xbufs[slot].at[pl.ds(pl.multiple_of(r * nch, nch), nch), :],
            gsem.at[slot])

    def scatter_copy(word, r, slot):
        dst = lax.shift_right_logical(word, tok_bits)
        return pltpu.make_async_copy(
            ybufs[slot].at[pl.ds(pl.multiple_of(r * nch, nch), nch), :],
            ys_hbm.at[pl.ds(pl.multiple_of(dst * nch, nch), nch), :],
            ssem.at[slot])

    def issue(copy, t, slot):
        def body(g, carry):
            for u in range(DMA_ISSUE_UNROLL):
                r = g * DMA_ISSUE_UNROLL + u
                copy(order_ref[t * tm + r], r, slot).start(priority=u % 2)
            return carry
        lax.fori_loop(0, tm // DMA_ISSUE_UNROLL, body, 0)

    def wait_all(copy, slot):
        if copy is gather_copy:
            whole = pltpu.make_async_copy(x_hbm.at[pl.ds(0, tm * nch), :], xbufs[slot], gsem.at[slot])
        else:
            whole = pltpu.make_async_copy(ybufs[slot], ys_hbm.at[pl.ds(0, tm * nch), :], ssem.at[slot])
        whole.wait()

    @pl.when(i == 0)
    def _():
        issue(gather_copy, 0, 0)

    @pl.when(valid & new_expert)
    def _():
        rchunk = 64
        def body(c, carry):
            r0 = pl.multiple_of(c * rchunk, rchunk)
            wgu_bf[pl.ds(r0, rchunk), :] = wgu_ref[pl.ds(r0, rchunk), :].astype(BF16)
            return carry
        lax.fori_loop(0, d // rchunk, body, 0)
        def body2(c, carry):
            r0 = pl.multiple_of(c * rchunk, rchunk)
            wdn_bf[pl.ds(r0, rchunk), :] = wdn_ref[pl.ds(r0, rchunk), :].astype(BF16)
            return carry
        lax.fori_loop(0, de // rchunk, body2, 0)

    for slot in (0, 1):
        @pl.when(valid & (tile % 2 == slot))
        def _(slot=slot):
            @pl.when(first)
            def _():
                @pl.when(tile + 1 < n_tiles)
                def _():
                    issue(gather_copy, tile + 1, 1 - slot)
                wait_all(gather_copy, slot)
                @pl.when(tile >= 2)
                def _():
                    wait_all(scatter_copy, slot)

            xbuf, ybuf = xbufs[slot], ybufs[slot]
            x = jnp.concatenate([xbuf[pl.ds(ch, tm, stride=nch), :] for ch in range(nch)], axis=1).astype(BF16)
            h = _bdot(x, wgu_bf[...]) + bgu_ref[...]
            gate = jnp.minimum(h[:, 0:de], SWIGLU_LIMIT)
            up = jnp.clip(h[:, de:2 * de], -SWIGLU_LIMIT, SWIGLU_LIMIT)
            act = (up + 1.0) * (gate * jax.nn.sigmoid(SWIGLU_ALPHA * gate))
            y = _bdot(act.astype(BF16), wdn_bf[...]) + bdn_ref[...]
            rows = lax.broadcasted_iota(jnp.int32, (tm, 1), 0)
            y = jnp.where((rows >= lo_ref[i]) & (rows < hi_ref[i]), y, 0.0)

            @pl.when(first)
            def _():
                for ch in range(nch):
                    ybuf[pl.ds(ch, tm, stride=nch), :] = y[:, ch * LANES:(ch + 1) * LANES]

            @pl.when(jnp.logical_not(first))
            def _():
                for ch in range(nch):
                    ybuf[pl.ds(ch, tm, stride=nch), :] = ybuf[pl.ds(ch, tm, stride=nch), :] + y[:, ch * LANES:(ch + 1) * LANES]

            @pl.when(last)
            def _():
                issue(scatter_copy, tile, slot)

    @pl.when(i == total - 1)
    def _():
        for tl in range(max(n_tiles - 2, 0), n_tiles):
            wait_all(scatter_copy, tl % 2)


def _moe(it_tile, it_exp, it_lo, it_hi, total, order, x1t, w_gu, b_gu, w_dn, b_dn, layer, tm, d):
    n_items = it_tile.shape[0]
    p0 = order.shape[0]
    n_tiles = p0 // tm
    de = w_dn.shape[2]
    nch = d // LANES
    wmap = lambda i, tl, ex, lo, hi, tot, od: (layer, ex[i], 0, 0)
    grid_spec = pltpu.PrefetchScalarGridSpec(
        num_scalar_prefetch=6,
        grid=(n_items,),
        in_specs=[pl.BlockSpec(memory_space=pl.ANY),
                  pl.BlockSpec((None, None, d, 2 * de), wmap), pl.BlockSpec((None, None, 1, 2 * de), wmap),
                  pl.BlockSpec((None, None, de, d), wmap), pl.BlockSpec((None, None, 1, d), wmap)],
        out_specs=pl.BlockSpec(memory_space=pl.ANY),
        scratch_shapes=[pltpu.VMEM((tm * nch, LANES), F32)] * 4 + [pltpu.SemaphoreType.DMA((2,))] * 2
                       + [pltpu.VMEM((d, 2 * de), BF16), pltpu.VMEM((de, d), BF16)],
    )
    return pl.pallas_call(
        functools.partial(_moe_body, tm=tm, d=d, de=de, nch=nch, n_tiles=n_tiles, n_items=n_items,
                          tok_bits=_tok_bits(x1t.shape[0] // nch)),
        grid_spec=grid_spec,
        out_shape=jax.ShapeDtypeStruct((p0 * nch, LANES), F32),
        compiler_params=_params(1),
        name="moe_experts",
    )(it_tile, it_exp, it_lo, it_hi, total, order, x1t, w_gu, b_gu, w_dn, b_dn)


def _combine_body(ys_ref, rg_ref, x1_ref, pp_ref, ps_ref, g2_ref, b2_ref, wpg_ref, bpg_ref, wp_ref, g3_ref, b3_ref,
                  op_ref, os_ref, *, tf, nch, alpha, n_pt):
    is_prompt = pl.program_id(0) < n_pt
    ple = jnp.where(is_prompt, pp_ref[...], ps_ref[...])
    rg = rg_ref[...]
    gk = [jnp.broadcast_to(rg[:, kk:kk + 1], (tf, LANES)) for kk in range(TOP_K)]
    fs, xs = [], []
    for ch in range(nch):
        f = gk[0] * ys_ref[pl.ds(ch, tf, stride=nch), :]
        for kk in range(1, TOP_K):
            f = f + gk[kk] * ys_ref[pl.ds(kk * tf * nch + ch, tf, stride=nch), :]
        fs.append(f)
        xs.append(x1_ref[pl.ds(ch, tf, stride=nch), :])
    f = jnp.concatenate(fs, axis=1)
    x1 = jnp.concatenate(xs, axis=1)
    x2 = _layer_norm(alpha * x1 + f, g2_ref[...], b2_ref[...])
    e = jax.nn.sigmoid(_bdot(x2.astype(BF16), wpg_ref[...]) + bpg_ref[...]) * _bdot(ple.astype(BF16), wp_ref[...])
    x3 = _layer_norm(alpha * x2 + e, g3_ref[...], b3_ref[...])

    @pl.when(is_prompt)
    def _():
        op_ref[...] = x3

    @pl.when(jnp.logical_not(is_prompt))
    def _():
        os_ref[...] = x3


def _combine(ys, rgate, x1t, p_p, p_s, g2, b2, wpg, bpg, wp, g3, b3, tf, d, alpha):
    tp, tsmp = p_p.shape[0], p_s.shape[0]
    nch = d // LANES
    n_pt = tp // tf
    prow, srow = _split_maps(n_pt)
    row = lambda i: (i, 0)
    fixed = lambda i: (0, 0)
    full = lambda a: pl.BlockSpec(a.shape, fixed)
    return pl.pallas_call(
        functools.partial(_combine_body, tf=tf, nch=nch, alpha=alpha, n_pt=n_pt),
        grid=((tp + tsmp) // tf,),
        in_specs=[pl.BlockSpec((tf * TOP_K * nch, LANES), row), pl.BlockSpec((tf, LANES), row),
                  pl.BlockSpec((tf * nch, LANES), row), pl.BlockSpec((tf, p_p.shape[1]), prow),
                  pl.BlockSpec((tf, p_s.shape[1]), srow)]
                 + [full(a) for a in (g2, b2, wpg, bpg, wp, g3, b3)],
        out_specs=[pl.BlockSpec((tf, d), prow), pl.BlockSpec((tf, d), srow)],
        out_shape=[jax.ShapeDtypeStruct((tp, d), F32), jax.ShapeDtypeStruct((tsmp, d), F32)],
        compiler_params=_params(1),
        name="combine_ple",
    )(ys, rgate, x1t, p_p, p_s, g2, b2, wpg, bpg, wp, g3, b3)


def kernel(x_prompt, x_sample, p_prompt, p_sample, cache_k, cache_v, state_conv, page_table, w_in, b_in, conv_w, conv_b, conv_ln_g, conv_ln_b, w_conv_out, w_attn_out, w_out, ln1_g, ln1_b, router_w, router_b, exp_w_gu, exp_b_gu, exp_w_down, exp_b_down, ln2_g, ln2_b, ple_w, ple_gate_w, ple_gate_b, ln3_g, ln3_b):
    bp, sp, d = x_prompt.shape
    bs, dec, _ = x_sample.shape
    depth = w_in.shape[0]
    d_conv = conv_w.shape[2]
    d_attn = w_attn_out.shape[1]
    nh = d_attn // HEAD_DIM
    n_exp = router_w.shape[2]
    tp, tsmp = bp * sp, bs * dec
    alpha = (2 * depth) ** 0.25
    assert 2 * n_exp <= LANES and d % LANES == 0

    ts = _pick_tile(512, tp, tsmp)
    tf = _pick_tile(256, tp, tsmp)
    tm = 256
    row2 = lambda a: a.reshape(1, -1)

    x_p, x_s = x_prompt.reshape(tp, d), x_sample.reshape(tsmp, d)
    slopes = jnp.asarray([2.0 ** (-8.0 * (h + 1) / nh) for h in range(nh)], dtype=F32)
    cache_kt = jnp.transpose(cache_k, (0, 1, 3, 4, 2))
    cache_vt = jnp.transpose(cache_v, (0, 1, 3, 4, 2))
    b_gu4 = exp_b_gu[:, :, None, :]
    b_dn4 = exp_b_down[:, :, None, :]

    k_p, v_p, c_p, k_s, v_s, c_s = [], [], [], [], [], []
    n_qkv = 2 * d_conv + 3 * d_attn
    for l in range(depth):
        u_p, q_p, k_pl, v_pl, u_s, q_s, k_sl, v_sl = _inproj(
            x_p, x_s, w_in[l][:, :n_qkv].astype(BF16), row2(b_in[l][:n_qkv]), ts, d_conv, d_attn)

        conv_args = (conv_w[l], row2(conv_b[l]), row2(conv_ln_g[l]), row2(conv_ln_b[l]))
        c_p_pre = _conv_prompt(u_p, *conv_args, bp, sp)
        u_s3 = u_s.reshape(bs, dec, d_conv)
        c_s_t = _conv_sample(jnp.transpose(u_s3, (1, 0, 2)), jnp.transpose(state_conv[l], (1, 0, 2)), *conv_args)
        c_s_pre = jnp.transpose(c_s_t, (1, 0, 2)).reshape(tsmp, d_conv)

        a_p_pre = _attn_prompt(slopes, q_p, k_pl, v_pl, bp, sp)
        a_s_pre = _attn_sample(slopes, page_table, q_s, k_sl, v_sl, cache_kt, cache_vt, l, bs, dec)

        rw_hi, rw_lo = _split_bf16(router_w[l])
        rw = jnp.zeros((d, LANES), BF16).at[:, :n_exp].set(rw_hi).at[:, n_exp:2 * n_exp].set(rw_lo)
        rb = jnp.zeros((1, LANES), F32).at[0, :n_exp].set(router_b[l])
        x1t, ridx, rgate = _merge(
            x_p, x_s, c_p_pre, c_s_pre, a_p_pre, a_s_pre, w_in[l][:, n_qkv:].astype(BF16), row2(b_in[l][n_qkv:]),
            w_conv_out[l].astype(BF16), w_attn_out[l].astype(BF16), w_out[l].astype(BF16),
            row2(ln1_g[l]), row2(ln1_b[l]), rw, rb, ts, alpha, n_exp)

        ys = _moe(*_route(ridx, n_exp, tm, tf), x1t, exp_w_gu, b_gu4, exp_w_down, b_dn4, l, tm, d)
        x_p, x_s = _combine(
            ys, rgate, x1t, p_prompt[l].reshape(tp, -1), p_sample[l].reshape(tsmp, -1), row2(ln2_g[l]), row2(ln2_b[l]),
            ple_gate_w[l].astype(BF16), row2(ple_gate_b[l]), ple_w[l].astype(BF16), row2(ln3_g[l]), row2(ln3_b[l]),
            tf, d, alpha)

        hist = conv_w.shape[1] - 1
        k_p.append(k_pl.reshape(bp, sp, nh, HEAD_DIM))
        v_p.append(v_pl.reshape(bp, sp, nh, HEAD_DIM))
        c_p.append(u_p.reshape(bp, sp, d_conv)[:, sp - hist:, :])
        k_s.append(k_sl.reshape(bs, dec, nh, HEAD_DIM))
        v_s.append(v_sl.reshape(bs, dec, nh, HEAD_DIM))
        c_s.append(jnp.concatenate([state_conv[l], u_s3], axis=1)[:, -hist:, :])

    return (x_p.reshape(bp, sp, d), x_s.reshape(bs, dec, d), jnp.stack(k_p), jnp.stack(v_p), jnp.stack(c_p),
            jnp.stack(k_s), jnp.stack(v_s), jnp.stack(c_s))
```
